```python
import jax, jax.numpy as jnp
from jax import lax
import numpy as np

D_MODEL = 1024
BATCH = 16
SEQ = 4096
DEPTH = 4

CHUNK = 128
GM_WIDTH = D_MODEL
GM_GROUPS = 8
GM_GROUP_DIM = GM_WIDTH // GM_GROUPS
CV_WIDTH = D_MODEL
CONV_K = 31
D_FF = (11 * D_MODEL) // 4
N_EXPERTS = 8
TOP_K = 2
D_FF_EXPERT = (7 * D_MODEL) // 2
N_DENSE = (DEPTH + 1) // 2
N_MOE = DEPTH // 2
IN_SPLITS = (GM_WIDTH, GM_WIDTH, CV_WIDTH, CV_WIDTH, D_MODEL, D_MODEL)
IN_WIDTH = sum(IN_SPLITS)
EPS = 1e-6

kernel_name = "hybrid_gmlp_conformer_moe_adaln"


def rms_norm(x):
    xf = x.astype(jnp.float32)
    y = xf * lax.rsqrt(jnp.mean(xf * xf, axis=-1, keepdims=True) + EPS)
    return y.astype(x.dtype)


def layer_norm(x, g, b):
    xf = x.astype(jnp.float32)
    mu = jnp.mean(xf, axis=-1, keepdims=True)
    var = jnp.mean(jnp.square(xf - mu), axis=-1, keepdims=True)
    y = (xf - mu) * lax.rsqrt(var + EPS)
    return (y * g.astype(jnp.float32) + b.astype(jnp.float32)).astype(x.dtype)


def ada_modulation(c, w, b):
    mod = jax.nn.silu(c) @ w + b
    return [m[:, None, :] for m in jnp.split(mod, 6, axis=-1)]


def hybrid_mixer(h, w_in, gm_ln_g, gm_ln_b, w_sp, b_sp, w_gm_out,
                 conv_w, conv_b, cv_ln_g, cv_ln_b, w_cv_out, w_o):
    bsz, s, _ = h.shape
    z = h @ w_in
    u, v, cv_a, cv_g, g_a, g_b = jnp.split(z, list(np.cumsum(IN_SPLITS)[:-1]), axis=-1)

    u = jax.nn.gelu(u)
    v = layer_norm(jax.nn.gelu(v), gm_ln_g, gm_ln_b)
    vc = v.reshape(bsz, s // CHUNK, CHUNK, GM_GROUPS, GM_GROUP_DIM)
    causal = jnp.tril(jnp.ones((CHUNK, CHUNK), dtype=bool))
    w_causal = jnp.where(causal[None], w_sp, jnp.zeros_like(w_sp))
    sv = jnp.einsum('gts,bnsgc->bntgc', w_causal, vc) + b_sp.T[:, :, None]
    y_a = (u * sv.reshape(bsz, s, GM_WIDTH)) @ w_gm_out

    a = cv_a * jax.nn.sigmoid(cv_g)
    a = lax.conv_general_dilated(
        a, conv_w[:, None, :], window_strides=(1,), padding=[(CONV_K - 1, 0)],
        dimension_numbers=('NWC', 'WIO', 'NWC'), feature_group_count=CV_WIDTH) + conv_b
    a = jax.nn.silu(layer_norm(a, cv_ln_g, cv_ln_b))
    y_b = a @ w_cv_out

    merged = jax.nn.sigmoid(g_a) * y_a + jax.nn.sigmoid(g_b) * y_b
    return merged @ w_o


def swiglu(h, w1, w3, w2):
    return (jax.nn.silu(h @ w1) * (h @ w3)) @ w2


def moe_swiglu(h, w_router, e_w1, e_w3, e_w2):
    bsz, s, d = h.shape
    t = h.reshape(bsz * s, d)
    logits = (t @ w_router).astype(jnp.float32)
    top_val, top_idx = lax.top_k(logits, TOP_K)
    top_w = jax.nn.softmax(top_val, axis=-1)
    combine = jnp.sum(jax.nn.one_hot(top_idx, N_EXPERTS, dtype=jnp.float32) * top_w[..., None], axis=1)
    combine = combine.astype(t.dtype)
    out = jnp.zeros_like(t)
    for e in range(N_EXPERTS):
        out = out + combine[:, e:e + 1] * swiglu(t, e_w1[e], e_w3[e], e_w2[e])
    return out.reshape(bsz, s, d)


def setup_inputs(seed: int = 0) -> dict:
    key = jax.random.key(seed)
    ks = jax.random.split(key, 24)

    def nrm(k, shape, scale):
        return jax.random.normal(k, shape, dtype=jnp.float32) * scale

    D = D_MODEL
    return {
        "x": nrm(ks[0], (BATCH, SEQ, D), 1.0),
        "c": nrm(ks[1], (BATCH, D), 1.0),
        "ada_w": nrm(ks[2], (DEPTH, D, 6 * D), D ** -0.5),
        "ada_b": nrm(ks[3], (DEPTH, 6 * D), 0.02),
        "w_in": nrm(ks[4], (DEPTH, D, IN_WIDTH), D ** -0.5),
        "gm_ln_g": 1.0 + nrm(ks[5], (DEPTH, GM_WIDTH), 0.02),
        "gm_ln_b": nrm(ks[6], (DEPTH, GM_WIDTH), 0.02),
        "w_sp": nrm(ks[7], (DEPTH, GM_GROUPS, CHUNK, CHUNK), CHUNK ** -0.5),
        "b_sp": 1.0 + nrm(ks[8], (DEPTH, GM_GROUPS, CHUNK), 0.02),
        "w_gm_out": nrm(ks[9], (DEPTH, GM_WIDTH, D), GM_WIDTH ** -0.5),
        "conv_w": nrm(ks[10], (DEPTH, CONV_K, CV_WIDTH), CONV_K ** -0.5),
        "conv_b": nrm(ks[11], (DEPTH, CV_WIDTH), 0.02),
        "cv_ln_g": 1.0 + nrm(ks[12], (DEPTH, CV_WIDTH), 0.02),
        "cv_ln_b": nrm(ks[13], (DEPTH, CV_WIDTH), 0.02),
        "w_cv_out": nrm(ks[14], (DEPTH, CV_WIDTH, D), CV_WIDTH ** -0.5),
        "w_o": nrm(ks[15], (DEPTH, D, D), D ** -0.5),
        "ffn_w1": nrm(ks[16], (N_DENSE, D, D_FF), D ** -0.5),
        "ffn_w3": nrm(ks[17], (N_DENSE, D, D_FF), D ** -0.5),
        "ffn_w2": nrm(ks[18], (N_DENSE, D_FF, D), D_FF ** -0.5),
        "router_w": nrm(ks[19], (N_MOE, D, N_EXPERTS), D ** -0.5),
        "exp_w1": nrm(ks[20], (N_MOE, N_EXPERTS, D, D_FF_EXPERT), D ** -0.5),
        "exp_w3": nrm(ks[21], (N_MOE, N_EXPERTS, D, D_FF_EXPERT), D ** -0.5),
        "exp_w2": nrm(ks[22], (N_MOE, N_EXPERTS, D_FF_EXPERT, D), D_FF_EXPERT ** -0.5),
        "final_g": 1.0 + nrm(ks[23], (D,), 0.02),
    }


def reference(x, c, ada_w, ada_b, w_in, gm_ln_g, gm_ln_b, w_sp, b_sp, w_gm_out,
              conv_w, conv_b, cv_ln_g, cv_ln_b, w_cv_out, w_o,
              ffn_w1, ffn_w3, ffn_w2, router_w, exp_w1, exp_w3, exp_w2, final_g):
    for l in range(DEPTH):
        sh1, sc1, g1, sh2, sc2, g2 = ada_modulation(c, ada_w[l], ada_b[l])
        h = rms_norm(x) * (1 + sc1) + sh1
        x = x + g1 * hybrid_mixer(h, w_in[l], gm_ln_g[l], gm_ln_b[l], w_sp[l], b_sp[l], w_gm_out[l],
                                  conv_w[l], conv_b[l], cv_ln_g[l], cv_ln_b[l], w_cv_out[l], w_o[l])
        h = rms_norm(x) * (1 + sc2) + sh2
        if l % 2 == 0:
            j = l // 2
            f = swiglu(h, ffn_w1[j], ffn_w3[j], ffn_w2[j])
        else:
            j = l // 2
            f = moe_swiglu(h, router_w[j], exp_w1[j], exp_w3[j], exp_w2[j])
        x = x + g2 * f
    return rms_norm(x) * final_g
```

```python
import functools

import jax
import jax.numpy as jnp
from jax import lax
from jax.experimental import pallas as pl
from jax.experimental.pallas import tpu as pltpu

F32 = jnp.float32
BF16 = jnp.bfloat16

EPS = 1e-6
CHUNK = 128
GM_GROUPS = 8
CONV_K = 31
N_EXPERTS = 8
HALO = 32
CONV_ROWS = 64
CONV_LANES = 256
VMEM_LIMIT_BYTES = 56 * 1024 * 1024


def _rows_loop(n_rows, rb, body):
    def step(i, carry):
        body(pl.multiple_of(i * rb, rb))
        return carry
    lax.fori_loop(0, n_rows // rb, step, 0)


def _const_spec(shape, index_map):
    return pl.BlockSpec(shape, index_map, pipeline_mode=pl.Buffered(1))


def _rms(x):
    return x * lax.rsqrt(jnp.mean(x * x, axis=-1, keepdims=True) + EPS)


def _layer_norm(x, g, b):
    mu = jnp.mean(x, axis=-1, keepdims=True)
    d = x - mu
    var = jnp.mean(d * d, axis=-1, keepdims=True)
    return d * lax.rsqrt(var + EPS) * g + b


def _ada_kernel(c_ref, w_ref, b_ref, o_ref):
    sc = jax.nn.silu(c_ref[...])
    o_ref[...] = jnp.dot(sc, w_ref[...], precision=lax.Precision.HIGHEST,
                         preferred_element_type=F32) + b_ref[...]


def _ada_modulation(c, ada_w, ada_b):
    depth, d, d6 = ada_w.shape
    bsz = c.shape[0]
    nj = d6 // d
    return pl.pallas_call(
        _ada_kernel,
        grid=(depth, nj),
        in_specs=[
            pl.BlockSpec((bsz, d), lambda l, j: (0, 0)),
            pl.BlockSpec((None, d, d), lambda l, j: (l, 0, j)),
            pl.BlockSpec((None, 1, d), lambda l, j: (l, 0, j)),
        ],
        out_specs=pl.BlockSpec((None, bsz, d), lambda l, j: (l, 0, j)),
        out_shape=jax.ShapeDtypeStruct((depth, bsz, d6), F32),
        compiler_params=pltpu.CompilerParams(
            dimension_semantics=("arbitrary", "arbitrary"), vmem_limit_bytes=VMEM_LIMIT_BYTES),
        name="ada_modulation",
    )(c, ada_w, ada_b.reshape(depth, 1, d6))


def _mixer_kernel(moe, ts, d, x_ref, mod_ref, win_ref, wsp_ref, bsp_ref, vec_ref, cw_ref,
                  wgm_ref, wcv_ref, wo_ref, *rest):
    if moe:
        (rw_ref, xo_ref, h2_ref, ti_ref, tw_ref,
         h_buf, z_buf, z2_buf, u_buf, v_buf, a_buf, ga_buf, gb_buf, ya_buf, yb_buf, wspm) = rest
    else:
        (xo_ref, h2_ref,
         h_buf, z_buf, z2_buf, u_buf, v_buf, a_buf, ga_buf, gb_buf, ya_buf, yb_buf, wspm) = rest
    rb = 32
    first_tile = jnp.logical_and(pl.program_id(0) == 0, pl.program_id(1) == 0)

    @pl.when(first_tile)
    def _():
        row = lax.broadcasted_iota(jnp.int32, (CHUNK, CHUNK), 0)
        col = lax.broadcasted_iota(jnp.int32, (CHUNK, CHUNK), 1)
        for g in range(GM_GROUPS):
            wspm[g] = jnp.where(row >= col, wsp_ref[g], 0.0).astype(BF16)

    @pl.when(pl.program_id(1) == 0)
    def _():
        a_buf[0:HALO, :] = jnp.zeros((HALO, d), F32)

    sh1, sc1, g1 = mod_ref[0:1, :], mod_ref[1:2, :], mod_ref[2:3, :]
    sh2, sc2 = mod_ref[3:4, :], mod_ref[4:5, :]

    def norm_body(r):
        xb = x_ref[pl.ds(r, rb), :]
        h_buf[pl.ds(r, rb), :] = (_rms(xb) * (1.0 + sc1) + sh1).astype(BF16)
    _rows_loop(ts, rb, norm_body)

    def in_proj(j):
        z_buf[...] = jnp.dot(h_buf[...], win_ref[:, j * d:(j + 1) * d], preferred_element_type=F32)

    in_proj(0)
    def u_body(r):
        u_buf[pl.ds(r, rb), :] = jax.nn.gelu(z_buf[pl.ds(r, rb), :]).astype(BF16)
    _rows_loop(ts, rb, u_body)

    in_proj(1)
    gm_g, gm_b = vec_ref[0:1, :], vec_ref[1:2, :]
    def v_body(r):
        v = jax.nn.gelu(z_buf[pl.ds(r, rb), :])
        v_buf[pl.ds(r, rb), :] = _layer_norm(v, gm_g, gm_b).astype(BF16)
    _rows_loop(ts, rb, v_body)

    in_proj(2)
    z2_buf[...] = jnp.dot(h_buf[...], win_ref[:, 3 * d:4 * d], preferred_element_type=F32)
    def a_body(r):
        a_buf[pl.ds(HALO + r, rb), :] = (
            z_buf[pl.ds(r, rb), :] * jax.nn.sigmoid(z2_buf[pl.ds(r, rb), :]))
    _rows_loop(ts, rb, a_body)

    in_proj(4)
    z2_buf[...] = jnp.dot(h_buf[...], win_ref[:, 5 * d:6 * d], preferred_element_type=F32)
    def gate_body(r):
        ga_buf[pl.ds(r, rb), :] = jax.nn.sigmoid(z_buf[pl.ds(r, rb), :]).astype(BF16)
        gb_buf[pl.ds(r, rb), :] = jax.nn.sigmoid(z2_buf[pl.ds(r, rb), :]).astype(BF16)
    _rows_loop(ts, rb, gate_body)

    gd = d // GM_GROUPS
    for ci in range(ts // CHUNK):
        rows = slice(ci * CHUNK, (ci + 1) * CHUNK)
        for g in range(GM_GROUPS):
            cols = slice(g * gd, (g + 1) * gd)
            sv = jnp.dot(wspm[g], v_buf[rows, cols], preferred_element_type=F32) + bsp_ref[:, cols]
            ya_buf[rows, cols] = (u_buf[rows, cols].astype(F32) * sv).astype(BF16)

    cv_g, cv_b, conv_b = vec_ref[2:3, :], vec_ref[3:4, :], vec_ref[4:5, :]
    off0 = HALO - CONV_K + 1
    def conv_body(r):
        for c0 in range(0, d, CONV_LANES):
            cols = slice(c0, c0 + CONV_LANES)
            win = a_buf[pl.ds(r, CONV_ROWS + HALO), cols]
            acc = jnp.broadcast_to(conv_b[:, cols], (CONV_ROWS, CONV_LANES))
            for rho in range(8):
                taps = [k for k in range(CONV_K) if (off0 + k) % 8 == rho]
                if not taps:
                    continue
                span = 8 * ((off0 + taps[-1]) // 8) + CONV_ROWS
                sh = win[rho:rho + span, :]
                for k in taps:
                    q8 = 8 * ((off0 + k) // 8)
                    acc = acc + cw_ref[k:k + 1, cols] * sh[q8:q8 + CONV_ROWS, :]
            z_buf[pl.ds(r, CONV_ROWS), cols] = acc
    _rows_loop(ts, CONV_ROWS, conv_body)
    a_buf[0:HALO, :] = a_buf[ts:ts + HALO, :]
    def cln_body(r):
        yb_buf[pl.ds(r, rb), :] = jax.nn.silu(
            _layer_norm(z_buf[pl.ds(r, rb), :], cv_g, cv_b)).astype(BF16)
    _rows_loop(ts, rb, cln_body)

    z_buf[...] = jnp.dot(ya_buf[...], wgm_ref[...], preferred_element_type=F32)
    z2_buf[...] = jnp.dot(yb_buf[...], wcv_ref[...], preferred_element_type=F32)
    def merge_body(r):
        m = (ga_buf[pl.ds(r, rb), :].astype(F32) * z_buf[pl.ds(r, rb), :]
             + gb_buf[pl.ds(r, rb), :].astype(F32) * z2_buf[pl.ds(r, rb), :])
        h_buf[pl.ds(r, rb), :] = m.astype(BF16)
    _rows_loop(ts, rb, merge_body)
    z_buf[...] = jnp.dot(h_buf[...], wo_ref[...], preferred_element_type=F32)

    def out_body(r):
        xn = x_ref[pl.ds(r, rb), :] + g1 * z_buf[pl.ds(r, rb), :]
        xo_ref[pl.ds(r, rb), :] = xn
        h2_ref[pl.ds(r, rb), :] = (_rms(xn) * (1.0 + sc2) + sh2).astype(h2_ref.dtype)
    _rows_loop(ts, rb, out_body)

    if moe:
        lg = lax.dot_general(rw_ref[...], h2_ref[...], (((1,), (1,)), ((), ())),
                             precision=lax.Precision.HIGHEST, preferred_element_type=F32)
        ls = [lg[e:e + 1, :] for e in range(N_EXPERTS)]
        m1 = functools.reduce(jnp.maximum, ls)
        i1 = jnp.full(m1.shape, N_EXPERTS - 1, jnp.int32)
        for e in range(N_EXPERTS - 2, -1, -1):
            i1 = jnp.where(ls[e] == m1, e, i1)
        ls2 = [jnp.where(i1 == e, -jnp.inf, ls[e]) for e in range(N_EXPERTS)]
        m2 = functools.reduce(jnp.maximum, ls2)
        i2 = jnp.full(m1.shape, N_EXPERTS - 1, jnp.int32)
        for e in range(N_EXPERTS - 2, -1, -1):
            i2 = jnp.where(ls2[e] == m2, e, i2)
        e2 = jnp.exp(m2 - m1)
        den = 1.0 + e2
        ti_ref[0:1, :] = i1
        ti_ref[1:2, :] = i2
        tw_ref[0:1, :] = 1.0 / den
        tw_ref[1:2, :] = e2 / den


def _mixer(l, moe, x, mod, win, wsp, bsp_full, vecs, cw, wgm, wcv, wo, rw_t):
    bsz, s, d = x.shape
    ts = min(512, s)
    ns = s // ts
    t = bsz * s
    h2_dtype = F32 if moe else BF16
    in_specs = [
        pl.BlockSpec((None, ts, d), lambda b, i: (b, i, 0)),
        pl.BlockSpec((None, None, 6, d), lambda b, i: (l, b, 0, 0)),
        _const_spec((None, d, 6 * d), lambda b, i: (l, 0, 0)),
        _const_spec((None, GM_GROUPS, CHUNK, CHUNK), lambda b, i: (l, 0, 0, 0)),
        _const_spec((None, CHUNK, d), lambda b, i: (l, 0, 0)),
        _const_spec((None, 8, d), lambda b, i: (l, 0, 0)),
        _const_spec((None, HALO, d), lambda b, i: (l, 0, 0)),
        _const_spec((None, d, d), lambda b, i: (l, 0, 0)),
        _const_spec((None, d, d), lambda b, i: (l, 0, 0)),
        _const_spec((None, d, d), lambda b, i: (l, 0, 0)),
    ]
    args = [x, mod, win, wsp, bsp_full, vecs, cw, wgm, wcv, wo]
    out_specs = [
        pl.BlockSpec((None, ts, d), lambda b, i: (b, i, 0)),
        pl.BlockSpec((None, ts, d), lambda b, i: (b, i, 0)),
    ]
    out_shape = [jax.ShapeDtypeStruct((bsz, s, d), F32), jax.ShapeDtypeStruct((bsz, s, d), h2_dtype)]
    if moe:
        in_specs.append(_const_spec((None, N_EXPERTS, d), lambda b, i: (l // 2, 0, 0)))
        args.append(rw_t)
        out_specs += [pl.BlockSpec((2, ts), lambda b, i: (0, b * ns + i)),
                      pl.BlockSpec((2, ts), lambda b, i: (0, b * ns + i))]
        out_shape += [jax.ShapeDtypeStruct((2, t), jnp.int32), jax.ShapeDtypeStruct((2, t), F32)]
    scratch = [
        pltpu.VMEM((ts, d), BF16),
        pltpu.VMEM((ts, d), F32),
        pltpu.VMEM((ts, d), F32),
        pltpu.VMEM((ts, d), BF16),
        pltpu.VMEM((ts, d), BF16),
        pltpu.VMEM((ts + HALO, d), F32),
        pltpu.VMEM((ts, d), BF16),
        pltpu.VMEM((ts, d), BF16),
        pltpu.VMEM((ts, d), BF16),
        pltpu.VMEM((ts, d), BF16),
        pltpu.VMEM((GM_GROUPS, CHUNK, CHUNK), BF16),
    ]
    return pl.pallas_call(
        functools.partial(_mixer_kernel, moe, ts, d),
        grid=(bsz, ns),
        in_specs=in_specs,
        out_specs=out_specs,
        out_shape=out_shape,
        scratch_shapes=scratch,
        compiler_params=pltpu.CompilerParams(
            dimension_semantics=("arbitrary", "arbitrary"), vmem_limit_bytes=VMEM_LIMIT_BYTES),
        name="mixer_moe" if moe else "mixer_dense",
    )(*args)


def _ffn_dense_kernel(tm, fc, nc, x_ref, h_ref, mod_ref, w1_ref, w3_ref, w2_ref, o_ref,
                      s1_buf, s3_buf, g_buf, acc_buf):
    rb = 32
    for c in range(nc):
        cols = slice(c * fc, (c + 1) * fc)
        s1_buf[...] = jnp.dot(h_ref[...], w1_ref[:, cols], preferred_element_type=F32)
        s3_buf[...] = jnp.dot(h_ref[...], w3_ref[:, cols], preferred_element_type=F32)
        def act_body(r):
            g_buf[pl.ds(r, rb), :] = (jax.nn.silu(s1_buf[pl.ds(r, rb), :])
                                      * s3_buf[pl.ds(r, rb), :]).astype(BF16)
        _rows_loop(tm, rb, act_body)
        part = jnp.dot(g_buf[...], w2_ref[cols, :], preferred_element_type=F32)
        if c == 0:
            acc_buf[...] = part
        else:
            acc_buf[...] += part
    g2 = mod_ref[5:6, :]
    def out_body(r):
        o_ref[pl.ds(r, rb), :] = x_ref[pl.ds(r, rb), :] + g2 * acc_buf[pl.ds(r, rb), :]
    _rows_loop(tm, rb, out_body)


def _ffn_dense(l, x, h2, mod, w1, w3, w2):
    t, d = x.shape
    f = w1.shape[-1]
    s = t // mod.shape[1]
    tm = min(512, s)
    nc = 2
    fc = f // nc
    j = l // 2
    return pl.pallas_call(
        functools.partial(_ffn_dense_kernel, tm, fc, nc),
        grid=(t // tm,),
        in_specs=[
            pl.BlockSpec((tm, d), lambda i: (i, 0)),
            pl.BlockSpec((tm, d), lambda i: (i, 0)),
            pl.BlockSpec((None, None, 6, d), lambda i: (l, (i * tm) // s, 0, 0)),
            _const_spec((None, d, f), lambda i: (j, 0, 0)),
            _const_spec((None, d, f), lambda i: (j, 0, 0)),
            _const_spec((None, f, d), lambda i: (j, 0, 0)),
        ],
        out_specs=pl.BlockSpec((tm, d), lambda i: (i, 0)),
        out_shape=jax.ShapeDtypeStruct((t, d), F32),
        scratch_shapes=[
            pltpu.VMEM((tm, fc), F32), pltpu.VMEM((tm, fc), F32),
            pltpu.VMEM((tm, fc), BF16), pltpu.VMEM((tm, d), F32),
        ],
        compiler_params=pltpu.CompilerParams(
            dimension_semantics=("arbitrary",), vmem_limit_bytes=VMEM_LIMIT_BYTES),
        name="ffn_dense",
    )(x, h2, mod, w1, w3, w2)


def _row_gather_copy(src_hbm, row, dst, dst_row, sem):
    return pltpu.make_async_copy(src_hbm.at[pl.ds(row, 1), :], dst.at[pl.ds(dst_row, 1), :], sem)


def _ffn_moe_kernel(tm, fc, nj, te_ref, nu_ref, src_hbm, h_hbm, w1_ref, w3_ref, w2_ref, y_ref,
                    idx_smem, xbuf, xbf, s1_buf, s3_buf, g_buf, acc_buf, idx_sem, row_sem):
    rb = 32
    i = pl.program_id(0)
    j = pl.program_id(1)
    n_used = nu_ref[0]
    slot = lax.rem(i, 2)

    def idx_copy(tile, sl):
        return pltpu.make_async_copy(src_hbm.at[tile], idx_smem.at[sl], idx_sem.at[sl])

    def start_rows(sl):
        def body(r, carry):
            _row_gather_copy(h_hbm, idx_smem[sl, r], xbuf.at[sl], r, row_sem.at[sl]).start()
            return carry
        lax.fori_loop(0, tm, body, 0, unroll=8)

    def wait_rows(sl):
        def body(r, carry):
            _row_gather_copy(h_hbm, 0, xbuf.at[sl], r, row_sem.at[sl]).wait()
            return carry
        lax.fori_loop(0, tm, body, 0, unroll=8)

    @pl.when(jnp.logical_and(i == 0, j == 0))
    def _():
        idx_copy(0, 0).start()
        idx_copy(0, 0).wait()
        start_rows(0)
        @pl.when(n_used > 1)
        def _():
            idx_copy(1, 1).start()

    @pl.when(jnp.logical_and(i < n_used, j == 0))
    def _():
        wait_rows(slot)
        @pl.when(i + 1 < n_used)
        def _():
            idx_copy(i + 1, 1 - slot).wait()
            start_rows(1 - slot)
        @pl.when(i + 2 < n_used)
        def _():
            idx_copy(i + 2, slot).start()
        def cast_body(r):
            xbf[pl.ds(r, rb), :] = xbuf[slot, pl.ds(r, rb), :].astype(BF16)
        _rows_loop(tm, rb, cast_body)

    @pl.when(i < n_used)
    def _():
        s1_buf[...] = jnp.dot(xbf[...], w1_ref[...], preferred_element_type=F32)
        s3_buf[...] = jnp.dot(xbf[...], w3_ref[...], preferred_element_type=F32)
        def act_body(r):
            g_buf[pl.ds(r, rb), :] = (jax.nn.silu(s1_buf[pl.ds(r, rb), :])
                                      * s3_buf[pl.ds(r, rb), :]).astype(BF16)
        _rows_loop(tm, rb, act_body)
        part = jnp.dot(g_buf[...], w2_ref[...], preferred_element_type=F32)
        @pl.when(j == 0)
        def _():
            acc_buf[...] = part
        @pl.when(j > 0)
        def _():
            acc_buf[...] += part
        @pl.when(j == nj - 1)
        def _():
            y_ref[...] = acc_buf[...]

    @pl.when(jnp.logical_and(i >= n_used, j == nj - 1))
    def _():
        y_ref[...] = jnp.zeros(y_ref.shape, F32)


def _ffn_moe(jl, h2, src, tile_expert, n_used, w1, w3, w2, tm, nt):
    t, d = h2.shape
    f = w1.shape[-1]
    fc = 512
    nj = f // fc
    grid_spec = pltpu.PrefetchScalarGridSpec(
        num_scalar_prefetch=2,
        grid=(nt, nj),
        in_specs=[
            pl.BlockSpec(memory_space=pl.ANY),
            pl.BlockSpec(memory_space=pl.ANY),
            pl.BlockSpec((None, None, d, fc), lambda i, j, te, nu: (jl, te[i], 0, j)),
            pl.BlockSpec((None, None, d, fc), lambda i, j, te, nu: (jl, te[i], 0, j)),
            pl.BlockSpec((None, None, fc, d), lambda i, j, te, nu: (jl, te[i], j, 0)),
        ],
        out_specs=pl.BlockSpec((tm, d), lambda i, j, te, nu: (i, 0)),
        scratch_shapes=[
            pltpu.SMEM((2, tm), jnp.int32),
            pltpu.VMEM((2, tm, d), F32),
            pltpu.VMEM((tm, d), BF16),
            pltpu.VMEM((tm, fc), F32), pltpu.VMEM((tm, fc), F32),
            pltpu.VMEM((tm, fc), BF16), pltpu.VMEM((tm, d), F32),
            pltpu.SemaphoreType.DMA((2,)), pltpu.SemaphoreType.DMA((2,)),
        ],
    )
    return pl.pallas_call(
        functools.partial(_ffn_moe_kernel, tm, fc, nj),
        grid_spec=grid_spec,
        out_shape=jax.ShapeDtypeStruct((nt * tm, d), F32),
        compiler_params=pltpu.CompilerParams(
            dimension_semantics=("arbitrary", "arbitrary"), vmem_limit_bytes=VMEM_LIMIT_BYTES,
            disable_bounds_checks=True),
        name="ffn_moe",
    )(tile_expert, n_used, src, h2, w1, w3, w2)


def _combine_kernel(tc, final, nt, slot_hbm, y_hbm, x_ref, tw_ref, mod_ref, fg_ref, o_ref,
                    idx_smem, ybuf, idx_sem, row_sem):
    i = pl.program_id(0)
    slot = lax.rem(i, 2)

    def idx_copy(tile, sl):
        return pltpu.make_async_copy(slot_hbm.at[tile], idx_smem.at[sl], idx_sem.at[sl])

    def start_rows(sl):
        def body(r, carry):
            _row_gather_copy(y_hbm, idx_smem[sl, r], ybuf.at[sl], r, row_sem.at[sl]).start()
            return carry
        lax.fori_loop(0, 2 * tc, body, 0, unroll=8)

    def wait_rows(sl):
        def body(r, carry):
            _row_gather_copy(y_hbm, 0, ybuf.at[sl], r, row_sem.at[sl]).wait()
            return carry
        lax.fori_loop(0, 2 * tc, body, 0, unroll=8)

    @pl.when(i == 0)
    def _():
        idx_copy(0, 0).start()
        idx_copy(0, 0).wait()
        start_rows(0)
        if nt > 1:
            idx_copy(1, 1).start()

    wait_rows(slot)
    @pl.when(i + 1 < nt)
    def _():
        idx_copy(i + 1, 1 - slot).wait()
        start_rows(1 - slot)
    @pl.when(i + 2 < nt)
    def _():
        idx_copy(i + 2, slot).start()

    g2 = mod_ref[5:6, :]
    tw = tw_ref[...]
    f = tw[:, 0:1] * ybuf[slot, 0:tc, :] + tw[:, 1:2] * ybuf[slot, tc:2 * tc, :]
    xn = x_ref[...] + g2 * f
    if final:
        xn = _rms(xn) * fg_ref[...]
    o_ref[...] = xn


def _combine(l, final, x, y, slots, tw_t, mod, final_g):
    t, d = x.shape
    s = t // mod.shape[1]
    tc = min(256, s)
    nt = t // tc
    slot_rows = slots.reshape(2, nt, tc).transpose(1, 0, 2).reshape(nt, 2 * tc)
    return pl.pallas_call(
        functools.partial(_combine_kernel, tc, final, nt),
        grid=(nt,),
        in_specs=[
            pl.BlockSpec(memory_space=pl.ANY),
            pl.BlockSpec(memory_space=pl.ANY),
            pl.BlockSpec((tc, d), lambda i: (i, 0)),
            pl.BlockSpec((tc, 2), lambda i: (i, 0)),
            pl.BlockSpec((None, None, 6, d), lambda i: (l, (i * tc) // s, 0, 0)),
            pl.BlockSpec((1, d), lambda i: (0, 0)),
        ],
        out_specs=pl.BlockSpec((tc, d), lambda i: (i, 0)),
        out_shape=jax.ShapeDtypeStruct((t, d), F32),
        scratch_shapes=[
            pltpu.SMEM((2, 2 * tc), jnp.int32),
            pltpu.VMEM((2, 2 * tc, d), F32),
            pltpu.SemaphoreType.DMA((2,)), pltpu.SemaphoreType.DMA((2,)),
        ],
        compiler_params=pltpu.CompilerParams(
            dimension_semantics=("arbitrary",), vmem_limit_bytes=VMEM_LIMIT_BYTES,
            disable_bounds_checks=True),
        name="moe_combine",
    )(slot_rows, y, x, tw_t, mod, final_g.reshape(1, d))


def _dispatch_plan(ti, tm, nt):
    t = ti.shape[1]
    e_flat = ti.reshape(-1)
    onehot = (e_flat[:, None] == jnp.arange(N_EXPERTS, dtype=jnp.int32)[None, :]).astype(jnp.int32)
    csum = jnp.cumsum(onehot, axis=0)
    counts = csum[-1]
    rank = jnp.take_along_axis(csum, e_flat[:, None], axis=1)[:, 0] - 1
    tiles_e = (counts + tm - 1) // tm
    tile_end = jnp.cumsum(tiles_e)
    tile_start = tile_end - tiles_e
    slots = tile_start[e_flat] * tm + rank
    n_used = tile_end[-1]
    tok = jnp.arange(2 * t, dtype=jnp.int32) % t
    src = jnp.zeros((nt * tm,), jnp.int32).at[slots].set(tok).reshape(nt, tm)
    tile_id = jnp.minimum(jnp.arange(nt, dtype=jnp.int32), n_used - 1)
    tile_expert = jnp.minimum(
        jnp.searchsorted(tile_end, tile_id, side="right").astype(jnp.int32), N_EXPERTS - 1)
    return slots.astype(jnp.int32), src, tile_expert, n_used.reshape(1).astype(jnp.int32)


def kernel(x, c, ada_w, ada_b, w_in, gm_ln_g, gm_ln_b, w_sp, b_sp, w_gm_out, conv_w, conv_b,
           cv_ln_g, cv_ln_b, w_cv_out, w_o, ffn_w1, ffn_w3, ffn_w2, router_w, exp_w1, exp_w3,
           exp_w2, final_g):
    bsz, s, d = x.shape
    depth = ada_w.shape[0]
    t = bsz * s
    assert s % CHUNK == 0 and d % GM_GROUPS == 0 and depth % 2 == 0

    mod = _ada_modulation(c, ada_w, ada_b).reshape(depth, bsz, 6, d)

    win = w_in.astype(BF16)
    wgm, wcv, wo = w_gm_out.astype(BF16), w_cv_out.astype(BF16), w_o.astype(BF16)
    bsp_full = jnp.repeat(jnp.swapaxes(b_sp, 1, 2), d // GM_GROUPS, axis=2)
    zeros = jnp.zeros_like(conv_b)
    vecs = jnp.stack([gm_ln_g, gm_ln_b, cv_ln_g, cv_ln_b, conv_b, zeros, zeros, zeros], axis=1)
    cw = jnp.pad(conv_w, ((0, 0), (0, HALO - CONV_K), (0, 0)))
    rw_t = jnp.swapaxes(router_w, 1, 2)
    fw1, fw3, fw2 = ffn_w1.astype(BF16), ffn_w3.astype(BF16), ffn_w2.astype(BF16)
    ew1, ew3, ew2 = exp_w1.astype(BF16), exp_w3.astype(BF16), exp_w2.astype(BF16)

    tm = min(1024, t)
    nt = (2 * t) // tm + N_EXPERTS

    for l in range(depth):
        moe = l % 2 == 1
        outs = _mixer(l, moe, x, mod, win, w_sp, bsp_full, vecs, cw, wgm, wcv, wo, rw_t)
        if not moe:
            xn, h2 = outs
            x2 = _ffn_dense(l, xn.reshape(t, d), h2.reshape(t, d), mod, fw1, fw3, fw2)
        else:
            xn, h2, ti, tw = outs
            slots, src, tile_expert, n_used = _dispatch_plan(ti, tm, nt)
            y = _ffn_moe(l // 2, h2.reshape(t, d), src, tile_expert, n_used, ew1, ew3, ew2, tm, nt)
            x2 = _combine(l, l == depth - 1, xn.reshape(t, d), y, slots, tw.T, mod, final_g)
        x = x2.reshape(bsz, s, d)
    return x
```

```python
import functools

import jax
import jax.numpy as jnp
from jax import lax
from jax.experimental import pallas as pl
from jax.experimental.pallas import tpu as pltpu

F32 = jnp.float32
BF16 = jnp.bfloat16

EPS = 1e-6
CHUNK = 128
GM_GROUPS = 8
CONV_K = 31
N_EXPERTS = 8
HALO = 32
CONV_ROWS = 64
CONV_LANES = 128
SUBLANES = 8
VMEM_LIMIT_BYTES = 56 * 1024 * 1024

MIX_ROWS = 512
FFN_ROWS = 512
FFN_COLS_DENSE = 256
FFN_COLS_MOE = 512
COMBINE_ROWS = 256


def _rows_loop(n_rows, rb, body, unroll=1):
    def step(i, carry):
        body(pl.multiple_of(i * rb, rb))
        return carry
    lax.fori_loop(0, n_rows // rb, step, 0, unroll=unroll)


def _const_spec(shape, index_map):
    return pl.BlockSpec(shape, index_map, pipeline_mode=pl.Buffered(1))


def _rms(x):
    return x * lax.rsqrt(jnp.mean(x * x, axis=-1, keepdims=True) + EPS)


def _layer_norm(x, g, b):
    mu = jnp.mean(x, axis=-1, keepdims=True)
    d = x - mu
    var = jnp.mean(d * d, axis=-1, keepdims=True)
    return d * lax.rsqrt(var + EPS) * g + b


def _ada_kernel(c_ref, w_ref, b_ref, o_ref):
    sc = jax.nn.silu(c_ref[...])
    o_ref[...] = jnp.dot(sc, w_ref[...], precision=lax.Precision.HIGHEST,
                         preferred_element_type=F32) + b_ref[...]


def _ada_modulation(c, ada_w, ada_b):
    depth, d, d6 = ada_w.shape
    bsz = c.shape[0]
    nj = d6 // d
    return pl.pallas_call(
        _ada_kernel,
        grid=(depth, nj),
        in_specs=[
            pl.BlockSpec((bsz, d), lambda l, j: (0, 0)),
            pl.BlockSpec((None, d, d), lambda l, j: (l, 0, j)),
            pl.BlockSpec((None, 1, d), lambda l, j: (l, 0, j)),
        ],
        out_specs=pl.BlockSpec((None, bsz, d), lambda l, j: (l, 0, j)),
        out_shape=jax.ShapeDtypeStruct((depth, bsz, d6), F32),
        compiler_params=pltpu.CompilerParams(
            dimension_semantics=("arbitrary", "arbitrary"), vmem_limit_bytes=VMEM_LIMIT_BYTES),
        name="ada_modulation",
    )(c, ada_w, ada_b.reshape(depth, 1, d6))


def _mixer_kernel(moe, ts, d, x_ref, mod_ref, win_ref, wsp_ref, bsp_ref, vec_ref, cw_ref,
                  wgm_ref, wcv_ref, wo_ref, *rest):
    if moe:
        (rw_ref, xo_ref, h2_ref, ti_ref, tw_ref,
         h_buf, z_buf, z2_buf, u_buf, v_buf, a_buf, ga_buf, gb_buf, ya_buf, yb_buf, wspm) = rest
    else:
        (xo_ref, h2_ref,
         h_buf, z_buf, z2_buf, u_buf, v_buf, a_buf, ga_buf, gb_buf, ya_buf, yb_buf, wspm) = rest
    rb = 32
    first_tile = jnp.logical_and(pl.program_id(0) == 0, pl.program_id(1) == 0)

    @pl.when(first_tile)
    def _():
        row = lax.broadcasted_iota(jnp.int32, (CHUNK, CHUNK), 0)
        col = lax.broadcasted_iota(jnp.int32, (CHUNK, CHUNK), 1)
        for g in range(GM_GROUPS):
            wspm[g] = jnp.where(row >= col, wsp_ref[g], 0.0).astype(BF16)

    @pl.when(pl.program_id(1) == 0)
    def _():
        a_buf[0:HALO, :] = jnp.zeros((HALO, d), F32)

    sh1, sc1, g1 = mod_ref[0:1, :], mod_ref[1:2, :], mod_ref[2:3, :]
    sh2, sc2 = mod_ref[3:4, :], mod_ref[4:5, :]

    def norm_body(r):
        xb = x_ref[pl.ds(r, rb), :]
        h_buf[pl.ds(r, rb), :] = (_rms(xb) * (1.0 + sc1) + sh1).astype(BF16)
    _rows_loop(ts, rb, norm_body, unroll=4)

    def in_proj(j):
        z_buf[...] = jnp.dot(h_buf[...], win_ref[:, j * d:(j + 1) * d], preferred_element_type=F32)

    in_proj(0)
    def u_body(r):
        u_buf[pl.ds(r, rb), :] = jax.nn.gelu(z_buf[pl.ds(r, rb), :]).astype(BF16)
    _rows_loop(ts, rb, u_body, unroll=2)

    in_proj(1)
    gm_g, gm_b = vec_ref[0:1, :], vec_ref[1:2, :]
    def v_body(r):
        v = jax.nn.gelu(z_buf[pl.ds(r, rb), :])
        v_buf[pl.ds(r, rb), :] = _layer_norm(v, gm_g, gm_b).astype(BF16)
    _rows_loop(ts, rb, v_body, unroll=4)

    in_proj(2)
    z2_buf[...] = jnp.dot(h_buf[...], win_ref[:, 3 * d:4 * d], preferred_element_type=F32)
    def a_body(r):
        a_buf[pl.ds(HALO + r, rb), :] = (
            z_buf[pl.ds(r, rb), :] * jax.nn.sigmoid(z2_buf[pl.ds(r, rb), :]))
    _rows_loop(ts, rb, a_body, unroll=2)

    in_proj(4)
    z2_buf[...] = jnp.dot(h_buf[...], win_ref[:, 5 * d:6 * d], preferred_element_type=F32)
    def gate_body(r):
        ga_buf[pl.ds(r, rb), :] = jax.nn.sigmoid(z_buf[pl.ds(r, rb), :]).astype(BF16)
        gb_buf[pl.ds(r, rb), :] = jax.nn.sigmoid(z2_buf[pl.ds(r, rb), :]).astype(BF16)
    _rows_loop(ts, rb, gate_body, unroll=2)

    gd = d // GM_GROUPS
    for ci in range(ts // CHUNK):
        rows = slice(ci * CHUNK, (ci + 1) * CHUNK)
        for g in range(GM_GROUPS):
            cols = slice(g * gd, (g + 1) * gd)
            sv = jnp.dot(wspm[g], v_buf[rows, cols], preferred_element_type=F32) + bsp_ref[:, cols]
            ya_buf[rows, cols] = (u_buf[rows, cols].astype(F32) * sv).astype(BF16)

    cv_g, cv_b, conv_b = vec_ref[2:3, :], vec_ref[3:4, :], vec_ref[4:5, :]
    off0 = HALO - CONV_K + 1
    win_rows = CONV_ROWS + HALO
    def conv_body(r):
        for c0 in range(0, d, CONV_LANES):
            cols = slice(c0, c0 + CONV_LANES)
            win = a_buf[pl.ds(r, win_rows), cols]
            acc = jnp.broadcast_to(conv_b[:, cols], (CONV_ROWS, CONV_LANES))
            for rho in range(SUBLANES):
                taps = [k for k in range(CONV_K) if (off0 + k) % SUBLANES == rho]
                if not taps:
                    continue
                sh = win if rho == 0 else pltpu.roll(win, win_rows - rho, axis=0)
                for k in taps:
                    q8 = (off0 + k) - rho
                    acc = acc + cw_ref[k:k + 1, cols] * sh[q8:q8 + CONV_ROWS, :]
            z_buf[pl.ds(r, CONV_ROWS), cols] = acc
    _rows_loop(ts, CONV_ROWS, conv_body)
    a_buf[0:HALO, :] = a_buf[ts:ts + HALO, :]
    def cln_body(r):
        yb_buf[pl.ds(r, rb), :] = jax.nn.silu(
            _layer_norm(z_buf[pl.ds(r, rb), :], cv_g, cv_b)).astype(BF16)
    _rows_loop(ts, rb, cln_body, unroll=4)

    z_buf[...] = jnp.dot(ya_buf[...], wgm_ref[...], preferred_element_type=F32)
    z2_buf[...] = jnp.dot(yb_buf[...], wcv_ref[...], preferred_element_type=F32)
    def merge_body(r):
        m = (ga_buf[pl.ds(r, rb), :].astype(F32) * z_buf[pl.ds(r, rb), :]
             + gb_buf[pl.ds(r, rb), :].astype(F32) * z2_buf[pl.ds(r, rb), :])
        h_buf[pl.ds(r, rb), :] = m.astype(BF16)
    _rows_loop(ts, rb, merge_body, unroll=2)
    z_buf[...] = jnp.dot(h_buf[...], wo_ref[...], preferred_element_type=F32)

    def out_body(r):
        xn = x_ref[pl.ds(r, rb), :] + g1 * z_buf[pl.ds(r, rb), :]
        xo_ref[pl.ds(r, rb), :] = xn
        h2_ref[pl.ds(r, rb), :] = (_rms(xn) * (1.0 + sc2) + sh2).astype(h2_ref.dtype)
    _rows_loop(ts, rb, out_body, unroll=4)

    if moe:
        lg = lax.dot_general(rw_ref[...], h2_ref[...], (((1,), (1,)), ((), ())),
                             precision=lax.Precision.HIGHEST, preferred_element_type=F32)
        ls = [lg[e:e + 1, :] for e in range(N_EXPERTS)]
        m1 = functools.reduce(jnp.maximum, ls)
        i1 = jnp.full(m1.shape, N_EXPERTS - 1, jnp.int32)
        for e in range(N_EXPERTS - 2, -1, -1):
            i1 = jnp.where(ls[e] == m1, e, i1)
        ls2 = [jnp.where(i1 == e, -jnp.inf, ls[e]) for e in range(N_EXPERTS)]
        m2 = functools.reduce(jnp.maximum, ls2)
        i2 = jnp.full(m1.shape, N_EXPERTS - 1, jnp.int32)
        for e in range(N_EXPERTS - 2, -1, -1):
            i2 = jnp.where(ls2[e] == m2, e, i2)
        e2 = jnp.exp(m2 - m1)
        den = 1.0 + e2
        ti_ref[0:1, :] = i1
        ti_ref[1:2, :] = i2
        tw_ref[0:1, :] = 1.0 / den
        tw_ref[1:2, :] = e2 / den


def _mixer(l, moe, x, mod, win, wsp, bsp_full, vecs, cw, wgm, wcv, wo, rw_t):
    bsz, s, d = x.shape
    ts = min(MIX_ROWS, s)
    ns = s // ts
    t = bsz * s
    h2_dtype = F32 if moe else BF16
    in_specs = [
        pl.BlockSpec((None, ts, d), lambda b, i: (b, i, 0)),
        pl.BlockSpec((None, None, 6, d), lambda b, i: (l, b, 0, 0)),
        _const_spec((None, d, 6 * d), lambda b, i: (l, 0, 0)),
        _const_spec((None, GM_GROUPS, CHUNK, CHUNK), lambda b, i: (l, 0, 0, 0)),
        _const_spec((None, CHUNK, d), lambda b, i: (l, 0, 0)),
        _const_spec((None, 8, d), lambda b, i: (l, 0, 0)),
        _const_spec((None, HALO, d), lambda b, i: (l, 0, 0)),
        _const_spec((None, d, d), lambda b, i: (l, 0, 0)),
        _const_spec((None, d, d), lambda b, i: (l, 0, 0)),
        _const_spec((None, d, d), lambda b, i: (l, 0, 0)),
    ]
    args = [x, mod, win, wsp, bsp_full, vecs, cw, wgm, wcv, wo]
    out_specs = [
        pl.BlockSpec((None, ts, d), lambda b, i: (b, i, 0)),
        pl.BlockSpec((None, ts, d), lambda b, i: (b, i, 0)),
    ]
    out_shape = [jax.ShapeDtypeStruct((bsz, s, d), F32), jax.ShapeDtypeStruct((bsz, s, d), h2_dtype)]
    if moe:
        in_specs.append(_const_spec((None, N_EXPERTS, d), lambda b, i: (l // 2, 0, 0)))
        args.append(rw_t)
        out_specs += [pl.BlockSpec((2, ts), lambda b, i: (0, b * ns + i)),
                      pl.BlockSpec((2, ts), lambda b, i: (0, b * ns + i))]
        out_shape += [jax.ShapeDtypeStruct((2, t), jnp.int32), jax.ShapeDtypeStruct((2, t), F32)]
    scratch = [
        pltpu.VMEM((ts, d), BF16),
        pltpu.VMEM((ts, d), F32),
        pltpu.VMEM((ts, d), F32),
        pltpu.VMEM((ts, d), BF16),
        pltpu.VMEM((ts, d), BF16),
        pltpu.VMEM((ts + HALO, d), F32),
        pltpu.VMEM((ts, d), BF16),
        pltpu.VMEM((ts, d), BF16),
        pltpu.VMEM((ts, d), BF16),
        pltpu.VMEM((ts, d), BF16),
        pltpu.VMEM((GM_GROUPS, CHUNK, CHUNK), BF16),
    ]
    return pl.pallas_call(
        functools.partial(_mixer_kernel, moe, ts, d),
        grid=(bsz, ns),
        in_specs=in_specs,
        out_specs=out_specs,
        out_shape=out_shape,
        scratch_shapes=scratch,
        compiler_params=pltpu.CompilerParams(
            dimension_semantics=("arbitrary", "arbitrary"), vmem_limit_bytes=VMEM_LIMIT_BYTES),
        name="mixer_moe" if moe else "mixer_dense",
    )(*args)


def _swiglu_tile(x_ref, w1_ref, w3_ref, w2_ref, g_buf, fc):
    f = w1_ref.shape[-1]
    x = x_ref[...]
    for c0 in range(0, f, fc):
        cols = slice(c0, c0 + fc)
        h1 = jnp.dot(x, w1_ref[:, cols], preferred_element_type=F32)
        h3 = jnp.dot(x, w3_ref[:, cols], preferred_element_type=F32)
        g_buf[:, cols] = (jax.nn.silu(h1) * h3).astype(BF16)
    return jnp.dot(g_buf[...], w2_ref[...], preferred_element_type=F32)


def _ffn_dense_kernel(fc, x_ref, h_ref, mod_ref, w1_ref, w3_ref, w2_ref, o_ref, g_buf):
    g2 = mod_ref[5:6, :]
    o_ref[...] = x_ref[...] + g2 * _swiglu_tile(h_ref, w1_ref, w3_ref, w2_ref, g_buf, fc)


def _ffn_dense(l, x, h2, mod, w1, w3, w2):
    t, d = x.shape
    f = w1.shape[-1]
    s = t // mod.shape[1]
    tm = min(FFN_ROWS, s)
    j = l // 2
    return pl.pallas_call(
        functools.partial(_ffn_dense_kernel, FFN_COLS_DENSE),
        grid=(t // tm,),
        in_specs=[
            pl.BlockSpec((tm, d), lambda i: (i, 0)),
            pl.BlockSpec((tm, d), lambda i: (i, 0)),
            pl.BlockSpec((None, None, 6, d), lambda i: (l, (i * tm) // s, 0, 0)),
            _const_spec((None, d, f), lambda i: (j, 0, 0)),
            _const_spec((None, d, f), lambda i: (j, 0, 0)),
            _const_spec((None, f, d), lambda i: (j, 0, 0)),
        ],
        out_specs=pl.BlockSpec((tm, d), lambda i: (i, 0)),
        out_shape=jax.ShapeDtypeStruct((t, d), F32),
        scratch_shapes=[pltpu.VMEM((tm, f), BF16)],
        compiler_params=pltpu.CompilerParams(
            dimension_semantics=("arbitrary",), vmem_limit_bytes=VMEM_LIMIT_BYTES),
        name="ffn_dense",
    )(x, h2, mod, w1, w3, w2)


def _row_gather_copy(src_hbm, row, dst, dst_row, sem):
    return pltpu.make_async_copy(src_hbm.at[pl.ds(row, 1), :], dst.at[pl.ds(dst_row, 1), :], sem)


def _start_row_gathers(src_hbm, idx_smem, sl, dst, n_rows, sem):
    def body(r, carry):
        _row_gather_copy(src_hbm, idx_smem[sl, r], dst, r, sem).start()
        return carry
    lax.fori_loop(0, n_rows, body, 0, unroll=8)


def _wait_row_gathers(src_hbm, dst, n_rows, sem):
    pltpu.make_async_copy(src_hbm.at[pl.ds(0, n_rows), :], dst, sem).wait()


def _ffn_moe_kernel(tm, fc, te_ref, nu_ref, src_hbm, h_hbm, w1_ref, w3_ref, w2_ref, y_ref,
                    idx_smem, xbuf, xbf, g_buf, idx_sem, row_sem):
    i = pl.program_id(0)
    n_used = nu_ref[0]
    slot = lax.rem(i, 2)

    def idx_copy(tile, sl):
        return pltpu.make_async_copy(src_hbm.at[tile], idx_smem.at[sl], idx_sem.at[sl])

    @pl.when(i == 0)
    def _():
        idx_copy(0, 0).start()
        idx_copy(0, 0).wait()
        _start_row_gathers(h_hbm, idx_smem, 0, xbuf.at[0], tm, row_sem.at[0])
        @pl.when(n_used > 1)
        def _():
            idx_copy(1, 1).start()

    @pl.when(i < n_used)
    def _():
        _wait_row_gathers(h_hbm, xbuf.at[slot], tm, row_sem.at[slot])
        @pl.when(i + 1 < n_used)
        def _():
            idx_copy(i + 1, 1 - slot).wait()
            _start_row_gathers(h_hbm, idx_smem, 1 - slot, xbuf.at[1 - slot], tm,
                               row_sem.at[1 - slot])
        @pl.when(i + 2 < n_used)
        def _():
            idx_copy(i + 2, slot).start()
        xbf[...] = xbuf[slot].astype(BF16)
        y_ref[...] = _swiglu_tile(xbf, w1_ref, w3_ref, w2_ref, g_buf, fc)

    @pl.when(i >= n_used)
    def _():
        y_ref[...] = jnp.zeros(y_ref.shape, F32)


def _ffn_moe(jl, h2, src, tile_expert, n_used, w1, w3, w2, tm, nt):
    t, d = h2.shape
    f = w1.shape[-1]
    grid_spec = pltpu.PrefetchScalarGridSpec(
        num_scalar_prefetch=2,
        grid=(nt,),
        in_specs=[
            pl.BlockSpec(memory_space=pl.ANY),
            pl.BlockSpec(memory_space=pl.ANY),
            _const_spec((None, None, d, f), lambda i, te, nu: (jl, te[i], 0, 0)),
            _const_spec((None, None, d, f), lambda i, te, nu: (jl, te[i], 0, 0)),
            _const_spec((None, None, f, d), lambda i, te, nu: (jl, te[i], 0, 0)),
        ],
        out_specs=pl.BlockSpec((tm, d), lambda i, te, nu: (i, 0)),
        scratch_shapes=[
            pltpu.SMEM((2, tm), jnp.int32),
            pltpu.VMEM((2, tm, d), F32),
            pltpu.VMEM((tm, d), BF16),
            pltpu.VMEM((tm, f), BF16),
            pltpu.SemaphoreType.DMA((2,)), pltpu.SemaphoreType.DMA((2,)),
        ],
    )
    return pl.pallas_call(
        functools.partial(_ffn_moe_kernel, tm, FFN_COLS_MOE),
        grid_spec=grid_spec,
        out_shape=jax.ShapeDtypeStruct((nt * tm, d), F32),
        compiler_params=pltpu.CompilerParams(
            dimension_semantics=("arbitrary",), vmem_limit_bytes=VMEM_LIMIT_BYTES,
            disable_bounds_checks=True),
        name="ffn_moe",
    )(tile_expert, n_used, src, h2, w1, w3, w2)


def _combine_kernel(tc, final, nt, slot_hbm, y_hbm, x_ref, tw_ref, mod_ref, fg_ref, o_ref,
                    idx_smem, ybuf, idx_sem, row_sem):
    i = pl.program_id(0)
    slot = lax.rem(i, 2)

    def idx_copy(tile, sl):
        return pltpu.make_async_copy(slot_hbm.at[tile], idx_smem.at[sl], idx_sem.at[sl])

    @pl.when(i == 0)
    def _():
        idx_copy(0, 0).start()
        idx_copy(0, 0).wait()
        _start_row_gathers(y_hbm, idx_smem, 0, ybuf.at[0], 2 * tc, row_sem.at[0])
        if nt > 1:
            idx_copy(1, 1).start()

    _wait_row_gathers(y_hbm, ybuf.at[slot], 2 * tc, row_sem.at[slot])
    @pl.when(i + 1 < nt)
    def _():
        idx_copy(i + 1, 1 - slot).wait()
        _start_row_gathers(y_hbm, idx_smem, 1 - slot, ybuf.at[1 - slot], 2 * tc,
                           row_sem.at[1 - slot])
    @pl.when(i + 2 < nt)
    def _():
        idx_copy(i + 2, slot).start()

    g2 = mod_ref[5:6, :]
    tw = tw_ref[...]
    f = tw[:, 0:1] * ybuf[slot, 0:tc, :] + tw[:, 1:2] * ybuf[slot, tc:2 * tc, :]
    xn = x_ref[...] + g2 * f
    if final:
        xn = _rms(xn) * fg_ref[...]
    o_ref[...] = xn


def _combine(l, final, x, y, slots, tw_t, mod, final_g):
    t, d = x.shape
    s = t // mod.shape[1]
    tc = min(COMBINE_ROWS, s)
    nt = t // tc
    slot_rows = slots.reshape(2, nt, tc).transpose(1, 0, 2).reshape(nt, 2 * tc)
    return pl.pallas_call(
        functools.partial(_combine_kernel, tc, final, nt),
        grid=(nt,),
        in_specs=[
            pl.BlockSpec(memory_space=pl.ANY),
            pl.BlockSpec(memory_space=pl.ANY),
            pl.BlockSpec((tc, d), lambda i: (i, 0)),
            pl.BlockSpec((tc, 2), lambda i: (i, 0)),
            pl.BlockSpec((None, None, 6, d), lambda i: (l, (i * tc) // s, 0, 0)),
            pl.BlockSpec((1, d), lambda i: (0, 0)),
        ],
        out_specs=pl.BlockSpec((tc, d), lambda i: (i, 0)),
        out_shape=jax.ShapeDtypeStruct((t, d), F32),
        scratch_shapes=[
            pltpu.SMEM((2, 2 * tc), jnp.int32),
            pltpu.VMEM((2, 2 * tc, d), F32),
            pltpu.SemaphoreType.DMA((2,)), pltpu.SemaphoreType.DMA((2,)),
        ],
        compiler_params=pltpu.CompilerParams(
            dimension_semantics=("arbitrary",), vmem_limit_bytes=VMEM_LIMIT_BYTES,
            disable_bounds_checks=True),
        name="moe_combine",
    )(slot_rows, y, x, tw_t, mod, final_g.reshape(1, d))


def _dispatch_plan(ti, tm, nt):
    t = ti.shape[1]
    e_flat = ti.reshape(-1)
    onehot = (e_flat[:, None] == jnp.arange(N_EXPERTS, dtype=jnp.int32)[None, :]).astype(jnp.int32)
    csum = jnp.cumsum(onehot, axis=0)
    counts = csum[-1]
    rank = jnp.take_along_axis(csum, e_flat[:, None], axis=1)[:, 0] - 1
    tiles_e = (counts + tm - 1) // tm
    tile_end = jnp.cumsum(tiles_e)
    tile_start = tile_end - tiles_e
    slots = tile_start[e_flat] * tm + rank
    n_used = tile_end[-1]
    tok = jnp.arange(2 * t, dtype=jnp.int32) % t
    src = jnp.zeros((nt * tm,), jnp.int32).at[slots].set(tok).reshape(nt, tm)
    tile_id = jnp.minimum(jnp.arange(nt, dtype=jnp.int32), n_used - 1)
    tile_expert = jnp.minimum(
        jnp.searchsorted(tile_end, tile_id, side="right").astype(jnp.int32), N_EXPERTS - 1)
    return slots.astype(jnp.int32), src, tile_expert, n_used.reshape(1).astype(jnp.int32)


def kernel(x, c, ada_w, ada_b, w_in, gm_ln_g, gm_ln_b, w_sp, b_sp, w_gm_out, conv_w, conv_b,
           cv_ln_g, cv_ln_b, w_cv_out, w_o, ffn_w1, ffn_w3, ffn_w2, router_w, exp_w1, exp_w3,
           exp_w2, final_g):
    bsz, s, d = x.shape
    depth = ada_w.shape[0]
    t = bsz * s
    assert s % CHUNK == 0 and d % GM_GROUPS == 0 and depth % 2 == 0

    mod = _ada_modulation(c, ada_w, ada_b).reshape(depth, bsz, 6, d)

    win = w_in.astype(BF16)
    wgm, wcv, wo = w_gm_out.astype(BF16), w_cv_out.astype(BF16), w_o.astype(BF16)
    bsp_full = jnp.repeat(jnp.swapaxes(b_sp, 1, 2), d // GM_GROUPS, axis=2)
    zeros = jnp.zeros_like(conv_b)
    vecs = jnp.stack([gm_ln_g, gm_ln_b, cv_ln_g, cv_ln_b, conv_b, zeros, zeros, zeros], axis=1)
    cw = jnp.pad(conv_w, ((0, 0), (0, HALO - CONV_K), (0, 0)))
    rw_t = jnp.swapaxes(router_w, 1, 2)
    fw1, fw3, fw2 = ffn_w1.astype(BF16), ffn_w3.astype(BF16), ffn_w2.astype(BF16)
    ew1, ew3, ew2 = exp_w1.astype(BF16), exp_w3.astype(BF16), exp_w2.astype(BF16)

    tm = min(FFN_ROWS, t)
    nt = (2 * t) // tm + N_EXPERTS

    for l in range(depth):
        moe = l % 2 == 1
        outs = _mixer(l, moe, x, mod, win, w_sp, bsp_full, vecs, cw, wgm, wcv, wo, rw_t)
        if not moe:
            xn, h2 = outs
            x2 = _ffn_dense(l, xn.reshape(t, d), h2.reshape(t, d), mod, fw1, fw3, fw2)
        else:
            xn, h2, ti, tw = outs
            slots, src, tile_expert, n_used = _dispatch_plan(ti, tm, nt)
            y = _ffn_moe(l // 2, h2.reshape(t, d), src, tile_expert, n_used, ew1, ew3, ew2, tm, nt)
            x2 = _combine(l, l == depth - 1, xn.reshape(t, d), y, slots, tw.T, mod, final_g)
        x = x2.reshape(bsz, s, d)
    return x
```

```python
import functools

import jax
import jax.numpy as jnp
from jax import lax
from jax.experimental import pallas as pl
from jax.experimental.pallas import tpu as pltpu

F32 = jnp.float32
BF16 = jnp.bfloat16

EPS = 1e-6
CHUNK = 128
GM_GROUPS = 8
CONV_K = 31
N_EXPERTS = 8
HALO = 32
CONV_ROWS = 64
CONV_LANES = 128
SUBLANES = 8
LANES = 128
VMEM_LIMIT_BYTES = 56 * 1024 * 1024

MIX_ROWS = 512
FFN_ROWS = 512
FFN_COLS_DENSE = 256
FFN_COLS_MOE = 512
COMBINE_ROWS = 256


def _const_spec(shape, index_map):
    return pl.BlockSpec(shape, index_map, pipeline_mode=pl.Buffered(1))


def _rms(x):
    return x * lax.rsqrt(jnp.mean(x * x, axis=-1, keepdims=True) + EPS)


def _layer_norm(x, g, b):
    mu = jnp.mean(x, axis=-1, keepdims=True)
    d = x - mu
    var = jnp.mean(d * d, axis=-1, keepdims=True)
    return d * lax.rsqrt(var + EPS) * g + b


def _ada_kernel(c_ref, w_ref, b_ref, o_ref):
    sc = jax.nn.silu(c_ref[...])
    o_ref[...] = jnp.dot(sc, w_ref[...], precision=lax.Precision.HIGHEST,
                         preferred_element_type=F32) + b_ref[...]


def _ada_modulation(c, ada_w, ada_b):
    depth, d, d6 = ada_w.shape
    bsz = c.shape[0]
    nj = d6 // d
    return pl.pallas_call(
        _ada_kernel,
        grid=(depth, nj),
        in_specs=[
            pl.BlockSpec((bsz, d), lambda l, j: (0, 0)),
            pl.BlockSpec((None, d, d), lambda l, j: (l, 0, j)),
            pl.BlockSpec((None, 1, d), lambda l, j: (l, 0, j)),
        ],
        out_specs=pl.BlockSpec((None, bsz, d), lambda l, j: (l, 0, j)),
        out_shape=jax.ShapeDtypeStruct((depth, bsz, d6), F32),
        compiler_params=pltpu.CompilerParams(
            dimension_semantics=("arbitrary", "arbitrary"), vmem_limit_bytes=VMEM_LIMIT_BYTES),
        name="ada_modulation",
    )(c, ada_w, ada_b.reshape(depth, 1, d6))


class _Interleaver:
    def __init__(self):
        self.items = []

    def add(self, name, unit, cost, deps, fn):
        self.items.append((name, unit, cost, tuple(deps), fn))

    def emit(self):
        done, unit_time, pending = {}, {"mxu": 0, "vpu": 0}, list(self.items)
        while pending:
            best = None
            for it in pending:
                if all(dep in done for dep in it[3]):
                    start = max([unit_time[it[1]]] + [done[dep] for dep in it[3]])
                    if best is None or start < best[0]:
                        best = (start, it)
            start, it = best
            it[4]()
            done[it[0]] = unit_time[it[1]] = start + it[2]
            pending.remove(it)


def _mixer_kernel(moe, ts, d, x_ref, mod_ref, win_ref, wsp_ref, bsp_ref, vec_ref, cwb_ref,
                  wgm_ref, wcv_ref, wo_ref, *rest):
    if moe:
        rw_ref, xo_ref, h2_ref, ti_ref, tw_ref = rest[:5]
        scratch = rest[5:]
    else:
        xo_ref, h2_ref = rest[:2]
        scratch = rest[2:]
    (h_buf, m_buf, z0, z1, z2, z3, zc, u_buf, v_buf, a_buf, ga_buf, gb_buf, ya_buf, yb_buf,
     wspm) = scratch[:15]
    h2f = scratch[15] if moe else None
    rb = 32
    qw = 256
    nq = d // qw
    row_blocks = list(range(0, ts, rb))
    first_tile = jnp.logical_and(pl.program_id(0) == 0, pl.program_id(1) == 0)

    @pl.when(first_tile)
    def _():
        row = lax.broadcasted_iota(jnp.int32, (CHUNK, CHUNK), 0)
        col = lax.broadcasted_iota(jnp.int32, (CHUNK, CHUNK), 1)
        for g in range(GM_GROUPS):
            wspm[g] = jnp.where(row >= col, wsp_ref[g], 0.0).astype(BF16)

    @pl.when(pl.program_id(1) == 0)
    def _():
        a_buf[0:HALO, :] = jnp.zeros((HALO, d), F32)

    sh1, sc1, g1 = mod_ref[0:1, :], mod_ref[1:2, :], mod_ref[2:3, :]
    sh2, sc2 = mod_ref[3:4, :], mod_ref[4:5, :]
    gm_g, gm_b = vec_ref[0:1, :], vec_ref[1:2, :]
    cv_g, cv_b, conv_b = vec_ref[2:3, :], vec_ref[3:4, :], vec_ref[4:5, :]
    gd = d // GM_GROUPS
    off0 = HALO - CONV_K + 1
    win_rows = CONV_ROWS + HALO
    sched = _Interleaver()

    def qcols(q):
        return slice(q * qw, (q + 1) * qw)

    def norm_item(r):
        def fn():
            xb = x_ref[r:r + rb, :]
            h_buf[r:r + rb, :] = (_rms(xb) * (1.0 + sc1) + sh1).astype(BF16)
        return fn
    for r in row_blocks:
        sched.add(("norm", r), "vpu", 80, [], norm_item(r))
    all_norm = [("norm", r) for r in row_blocks]

    def in_dot_item(split, q, dst):
        def fn():
            c0 = split * d + q * qw
            dst[:, qcols(q)] = jnp.dot(h_buf[...], win_ref[:, c0:c0 + qw], preferred_element_type=F32)
        return fn

    def a_item(q):
        def fn():
            a_buf[HALO:HALO + ts, qcols(q)] = z0[:, qcols(q)] * jax.nn.sigmoid(z1[:, qcols(q)])
        return fn

    def conv_item(r, q):
        def fn():
            for c0 in range(q * qw, (q + 1) * qw, CONV_LANES):
                cols = slice(c0, c0 + CONV_LANES)
                win = a_buf[r:r + win_rows, cols]
                acc = jnp.broadcast_to(conv_b[:, cols], (CONV_ROWS, CONV_LANES))
                acc = acc.reshape(CONV_ROWS // SUBLANES, SUBLANES, CONV_LANES)
                for rho in range(SUBLANES):
                    taps = [k for k in range(CONV_K) if (off0 + k) % SUBLANES == rho]
                    if not taps:
                        continue
                    sh = win if rho == 0 else pltpu.roll(win, win_rows - rho, axis=0)
                    for k in taps:
                        q8 = (off0 + k) - rho
                        w_k = cwb_ref[k * SUBLANES:(k + 1) * SUBLANES, cols]
                        acc = acc + w_k[None] * sh[q8:q8 + CONV_ROWS, :].reshape(acc.shape)
                zc[r:r + CONV_ROWS, cols] = acc.reshape(CONV_ROWS, CONV_LANES)
        return fn

    def halo_item():
        a_buf[0:HALO, :] = a_buf[ts:ts + HALO, :]

    def cln_item(r):
        def fn():
            yb_buf[r:r + rb, :] = jax.nn.silu(_layer_norm(zc[r:r + rb, :], cv_g, cv_b)).astype(BF16)
        return fn

    def u_item(q):
        def fn():
            u_buf[:, qcols(q)] = jax.nn.gelu(z2[:, qcols(q)]).astype(BF16)
        return fn

    def v_item(r):
        def fn():
            v = jax.nn.gelu(z3[r:r + rb, :])
            v_buf[r:r + rb, :] = _layer_norm(v, gm_g, gm_b).astype(BF16)
        return fn

    def gate_item(q):
        def fn():
            ga_buf[:, qcols(q)] = jax.nn.sigmoid(z0[:, qcols(q)]).astype(BF16)
            gb_buf[:, qcols(q)] = jax.nn.sigmoid(z1[:, qcols(q)]).astype(BF16)
        return fn

    def spatial_item(ci):
        def fn():
            rows = slice(ci * CHUNK, (ci + 1) * CHUNK)
            for g in range(GM_GROUPS):
                cols = slice(g * gd, (g + 1) * gd)
                sv = jnp.dot(wspm[g], v_buf[rows, cols], preferred_element_type=F32) + bsp_ref[:, cols]
                ya_buf[rows, cols] = (u_buf[rows, cols].astype(F32) * sv).astype(BF16)
        return fn

    def out_dot_item(src, w_ref, dst, q, rows=slice(None)):
        def fn():
            dst[rows, qcols(q)] = jnp.dot(src[rows, :], w_ref[:, qcols(q)], preferred_element_type=F32)
        return fn

    def merge_item(q):
        def fn():
            m = (ga_buf[:, qcols(q)].astype(F32) * z2[:, qcols(q)]
                 + gb_buf[:, qcols(q)].astype(F32) * z3[:, qcols(q)])
            m_buf[:, qcols(q)] = m.astype(BF16)
        return fn

    def out_item(r):
        def fn():
            xn = x_ref[r:r + rb, :] + g1 * z0[r:r + rb, :]
            xo_ref[r:r + rb, :] = xn
            h2 = _rms(xn) * (1.0 + sc2) + sh2
            if moe:
                h2f[r:r + rb, :] = h2
                for s8 in range(d // LANES):
                    h2_ref[pl.ds(r * SUBLANES + s8, rb, stride=SUBLANES), :] = (
                        h2[:, s8 * LANES:(s8 + 1) * LANES])
            else:
                h2_ref[r:r + rb, :] = h2.astype(h2_ref.dtype)
        return fn

    conv_blocks = list(range(0, ts, CONV_ROWS))
    chunks = list(range(ts // CHUNK))
    halves = [slice(0, ts // 2), slice(ts // 2, ts)]
    for q in range(nq):
        sched.add(("d_cva", q), "mxu", 512, all_norm, in_dot_item(2, q, z0))
        sched.add(("d_cvg", q), "mxu", 512, all_norm, in_dot_item(3, q, z1))
        sched.add(("a", q), "vpu", 300, [("d_cva", q), ("d_cvg", q)], a_item(q))
    for q in range(nq):
        sched.add(("d_u", q), "mxu", 512, all_norm, in_dot_item(0, q, z2))
        sched.add(("u", q), "vpu", 350, [("d_u", q)], u_item(q))
    for q in range(nq):
        sched.add(("d_v", q), "mxu", 512, all_norm, in_dot_item(1, q, z3))
    for r in row_blocks:
        sched.add(("v", r), "vpu", 170, [("d_v", q) for q in range(nq)], v_item(r))
    for q in range(nq):
        sched.add(("d_ga", q), "mxu", 512, all_norm + [("a", q)], in_dot_item(4, q, z0))
        sched.add(("d_gb", q), "mxu", 512, all_norm + [("a", q)], in_dot_item(5, q, z1))
        sched.add(("g", q), "vpu", 550, [("d_ga", q), ("d_gb", q)], gate_item(q))
    for ci in chunks:
        deps = [("v", r) for r in row_blocks if r // CHUNK == ci] + [("u", q) for q in range(nq)]
        sched.add(("sp", ci), "mxu", 600, deps, spatial_item(ci))
    for q in range(nq):
        for r in conv_blocks:
            sched.add(("conv", r, q), "vpu", 370, [("a", q)], conv_item(r, q))
    for r in row_blocks:
        r0 = (r // CONV_ROWS) * CONV_ROWS
        sched.add(("cln", r), "vpu", 170, [("conv", r0, q) for q in range(nq)], cln_item(r))
    sched.add("halo", "vpu", 10,
              [("conv", 0, q) for q in range(nq)] + [("a", q) for q in range(nq)], halo_item)
    all_sp = [("sp", ci) for ci in chunks]
    all_cln = [("cln", r) for r in row_blocks]
    all_v = [("v", r) for r in row_blocks]
    for q in range(nq):
        sched.add(("d_ya", q), "mxu", 512, all_sp, out_dot_item(ya_buf, wgm_ref, z2, q))
        sched.add(("d_yb", q), "mxu", 512, all_cln + all_v, out_dot_item(yb_buf, wcv_ref, z3, q))
        sched.add(("m", q), "vpu", 250, [("d_ya", q), ("d_yb", q), ("g", q)], merge_item(q))
    all_m = [("m", q) for q in range(nq)]
    for hi, rows in enumerate(halves):
        for q in range(nq):
            sched.add(("d_o", hi, q), "mxu", 256, all_m, out_dot_item(m_buf, wo_ref, z0, q, rows))
        for r in row_blocks:
            if rows.start <= r < rows.stop:
                sched.add(("out", r), "vpu", 100, [("d_o", hi, q) for q in range(nq)], out_item(r))
    sched.emit()

    if moe:
        lg = lax.dot_general(rw_ref[...], h2f[...], (((1,), (1,)), ((), ())),
                             precision=lax.Precision.HIGHEST, preferred_element_type=F32)
        ls = [lg[e:e + 1, :] for e in range(N_EXPERTS)]
        m1 = functools.reduce(jnp.maximum, ls)
        i1 = jnp.full(m1.shape, N_EXPERTS - 1, jnp.int32)
        for e in range(N_EXPERTS - 2, -1, -1):
            i1 = jnp.where(ls[e] == m1, e, i1)
        ls2 = [jnp.where(i1 == e, -jnp.inf, ls[e]) for e in range(N_EXPERTS)]
        m2 = functools.reduce(jnp.maximum, ls2)
        i2 = jnp.full(m1.shape, N_EXPERTS - 1, jnp.int32)
        for e in range(N_EXPERTS - 2, -1, -1):
            i2 = jnp.where(ls2[e] == m2, e, i2)
        e2 = jnp.exp(m2 - m1)
        den = 1.0 + e2
        ti_ref[0:1, :] = i1
        ti_ref[1:2, :] = i2
        tw_ref[0:1, :] = 1.0 / den
        tw_ref[1:2, :] = e2 / den


def _mixer(l, moe, x, mod, win, wsp, bsp_full, vecs, cw, wgm, wcv, wo, rw_t):
    bsz, s, d = x.shape
    ts = min(MIX_ROWS, s)
    ns = s // ts
    t = bsz * s
    in_specs = [
        pl.BlockSpec((None, ts, d), lambda b, i: (b, i, 0)),
        pl.BlockSpec((None, None, 6, d), lambda b, i: (l, b, 0, 0)),
        _const_spec((None, d, 6 * d), lambda b, i: (l, 0, 0)),
        _const_spec((None, GM_GROUPS, CHUNK, CHUNK), lambda b, i: (l, 0, 0, 0)),
        _const_spec((None, CHUNK, d), lambda b, i: (l, 0, 0)),
        _const_spec((None, 8, d), lambda b, i: (l, 0, 0)),
        _const_spec((None, HALO * SUBLANES, d), lambda b, i: (l, 0, 0)),
        _const_spec((None, d, d), lambda b, i: (l, 0, 0)),
        _const_spec((None, d, d), lambda b, i: (l, 0, 0)),
        _const_spec((None, d, d), lambda b, i: (l, 0, 0)),
    ]
    args = [x, mod, win, wsp, bsp_full, vecs, cw, wgm, wcv, wo]
    tok_rows = d // LANES
    assert tok_rows == SUBLANES
    out_specs = [pl.BlockSpec((None, ts, d), lambda b, i: (b, i, 0))]
    out_shape = [jax.ShapeDtypeStruct((bsz, s, d), F32)]
    if moe:
        out_specs.append(pl.BlockSpec((ts * tok_rows, LANES), lambda b, i: (b * ns + i, 0)))
        out_shape.append(jax.ShapeDtypeStruct((t * tok_rows, LANES), F32))
    else:
        out_specs.append(pl.BlockSpec((None, ts, d), lambda b, i: (b, i, 0)))
        out_shape.append(jax.ShapeDtypeStruct((bsz, s, d), BF16))
    if moe:
        in_specs.append(_const_spec((None, N_EXPERTS, d), lambda b, i: (l // 2, 0, 0)))
        args.append(rw_t)
        out_specs += [pl.BlockSpec((2, ts), lambda b, i: (0, b * ns + i)),
                      pl.BlockSpec((2, ts), lambda b, i: (0, b * ns + i))]
        out_shape += [jax.ShapeDtypeStruct((2, t), jnp.int32), jax.ShapeDtypeStruct((2, t), F32)]
    scratch = [
        pltpu.VMEM((ts, d), BF16),
        pltpu.VMEM((ts, d), BF16),
        pltpu.VMEM((ts, d), F32),
        pltpu.VMEM((ts, d), F32),
        pltpu.VMEM((ts, d), F32),
        pltpu.VMEM((ts, d), F32),
        pltpu.VMEM((ts, d), F32),
        pltpu.VMEM((ts, d), BF16),
        pltpu.VMEM((ts, d), BF16),
        pltpu.VMEM((ts + HALO, d), F32),
        pltpu.VMEM((ts, d), BF16),
        pltpu.VMEM((ts, d), BF16),
        pltpu.VMEM((ts, d), BF16),
        pltpu.VMEM((ts, d), BF16),
        pltpu.VMEM((GM_GROUPS, CHUNK, CHUNK), BF16),
    ]
    if moe:
        scratch.append(pltpu.VMEM((ts, d), F32))
    return pl.pallas_call(
        functools.partial(_mixer_kernel, moe, ts, d),
        grid=(bsz, ns),
        in_specs=in_specs,
        out_specs=out_specs,
        out_shape=out_shape,
        scratch_shapes=scratch,
        compiler_params=pltpu.CompilerParams(
            dimension_semantics=("arbitrary", "arbitrary"), vmem_limit_bytes=VMEM_LIMIT_BYTES),
        name="mixer_moe" if moe else "mixer_dense",
    )(*args)


def _swiglu_tile(x_ref, w1_ref, w3_ref, w2_ref, g_buf, fc):
    f = w1_ref.shape[-1]
    x = x_ref[...]
    for c0 in range(0, f, fc):
        cols = slice(c0, c0 + fc)
        h1 = jnp.dot(x, w1_ref[:, cols], preferred_element_type=F32)
        h3 = jnp.dot(x, w3_ref[:, cols], preferred_element_type=F32)
        g_buf[:, cols] = (jax.nn.silu(h1) * h3).astype(BF16)
    return jnp.dot(g_buf[...], w2_ref[...], preferred_element_type=F32)


def _ffn_dense_kernel(fc, x_ref, h_ref, mod_ref, w1_ref, w3_ref, w2_ref, o_ref, g_buf):
    g2 = mod_ref[5:6, :]
    o_ref[...] = x_ref[...] + g2 * _swiglu_tile(h_ref, w1_ref, w3_ref, w2_ref, g_buf, fc)


def _ffn_dense(l, x, h2, mod, w1, w3, w2):
    t, d = x.shape
    f = w1.shape[-1]
    s = t // mod.shape[1]
    tm = min(FFN_ROWS, s)
    j = l // 2
    return pl.pallas_call(
        functools.partial(_ffn_dense_kernel, FFN_COLS_DENSE),
        grid=(t // tm,),
        in_specs=[
            pl.BlockSpec((tm, d), lambda i: (i, 0)),
            pl.BlockSpec((tm, d), lambda i: (i, 0)),
            pl.BlockSpec((None, None, 6, d), lambda i: (l, (i * tm) // s, 0, 0)),
            _const_spec((None, d, f), lambda i: (j, 0, 0)),
            _const_spec((None, d, f), lambda i: (j, 0, 0)),
            _const_spec((None, f, d), lambda i: (j, 0, 0)),
        ],
        out_specs=pl.BlockSpec((tm, d), lambda i: (i, 0)),
        out_shape=jax.ShapeDtypeStruct((t, d), F32),
        scratch_shapes=[pltpu.VMEM((tm, f), BF16)],
        compiler_params=pltpu.CompilerParams(
            dimension_semantics=("arbitrary",), vmem_limit_bytes=VMEM_LIMIT_BYTES),
        name="ffn_dense",
    )(x, h2, mod, w1, w3, w2)


def _token_copy(src, src_row, dst, dst_row, sem):
    return pltpu.make_async_copy(src.at[pl.ds(pl.multiple_of(src_row, SUBLANES), SUBLANES), :],
                                 dst.at[pl.ds(pl.multiple_of(dst_row, SUBLANES), SUBLANES), :], sem)


def _ffn_moe_kernel(tm, fc, te_ref, nu_ref, nv_ref, idx_hbm, h_hbm, w1_ref, w3_ref, w2_ref, y_hbm,
                    idx_smem, xbuf, xbf, g_buf, ybuf, idx_sem, in_sem, out_sem):
    i = pl.program_id(0)
    n_used = nu_ref[0]
    slot = lax.rem(i, 2)
    d = xbf.shape[1]

    def idx_copy(tile, sl):
        return pltpu.make_async_copy(idx_hbm.at[tile], idx_smem.at[sl], idx_sem.at[sl])

    def start_gathers(sl):
        def body(r, carry):
            _token_copy(h_hbm, idx_smem[sl, r], xbuf.at[sl], r * SUBLANES, in_sem.at[sl]).start()
            return carry
        lax.fori_loop(0, tm, body, 0, unroll=8)

    def wait_gathers(sl):
        pltpu.make_async_copy(h_hbm.at[pl.ds(0, tm * SUBLANES), :], xbuf.at[sl], in_sem.at[sl]).wait()

    def scatter_copy(sl, r):
        return _token_copy(ybuf.at[sl], r * SUBLANES, y_hbm, idx_smem[sl, tm + r], out_sem.at[sl])

    def start_scatters(sl, n_valid):
        @pl.when(n_valid == tm)
        def _():
            def body(r, carry):
                scatter_copy(sl, r).start()
                return carry
            lax.fori_loop(0, tm, body, 0, unroll=8)
        @pl.when(n_valid < tm)
        def _():
            def body(r, carry):
                scatter_copy(sl, r).start()
                return carry
            lax.fori_loop(0, n_valid, body, 0)

    def wait_scatters(sl, n_valid):
        @pl.when(n_valid == tm)
        def _():
            pltpu.make_async_copy(ybuf.at[sl], y_hbm.at[pl.ds(0, tm * SUBLANES), :], out_sem.at[sl]).wait()
        @pl.when(n_valid < tm)
        def _():
            def body(r, carry):
                _token_copy(ybuf.at[sl], r * SUBLANES, y_hbm, 0, out_sem.at[sl]).wait()
                return carry
            lax.fori_loop(0, n_valid, body, 0)

    @pl.when(i == 0)
    def _():
        idx_copy(0, 0).start()
        idx_copy(0, 0).wait()
        start_gathers(0)
        @pl.when(n_used > 1)
        def _():
            idx_copy(1, 1).start()

    @pl.when(i < n_used)
    def _():
        wait_gathers(slot)
        @pl.when(i + 1 < n_used)
        def _():
            idx_copy(i + 1, 1 - slot).wait()
            start_gathers(1 - slot)
        for s8 in range(d // LANES):
            xbf[:, s8 * LANES:(s8 + 1) * LANES] = (
                xbuf[slot, pl.ds(s8, tm, stride=SUBLANES), :].astype(BF16))
        y = _swiglu_tile(xbf, w1_ref, w3_ref, w2_ref, g_buf, fc)
        @pl.when(i >= 2)
        def _():
            wait_scatters(slot, nv_ref[jnp.maximum(i - 2, 0)])
        for s8 in range(d // LANES):
            ybuf[slot, pl.ds(s8, tm, stride=SUBLANES), :] = y[:, s8 * LANES:(s8 + 1) * LANES]
        start_scatters(slot, nv_ref[i])
        @pl.when(i + 2 < n_used)
        def _():
            idx_copy(i + 2, slot).start()
        @pl.when(i == n_used - 1)
        def _():
            @pl.when(i >= 1)
            def _():
                wait_scatters(1 - slot, nv_ref[jnp.maximum(i - 1, 0)])
            wait_scatters(slot, nv_ref[i])


def _ffn_moe(jl, h2t, idx, tile_expert, n_used, n_valid, w1, w3, w2, tm, nt, n_out_rows):
    d, f = w1.shape[-2:]
    grid_spec = pltpu.PrefetchScalarGridSpec(
        num_scalar_prefetch=3,
        grid=(nt,),
        in_specs=[
            pl.BlockSpec(memory_space=pl.ANY),
            pl.BlockSpec(memory_space=pl.ANY),
            _const_spec((None, None, d, f), lambda i, te, nu, nv: (jl, te[i], 0, 0)),
            _const_spec((None, None, d, f), lambda i, te, nu, nv: (jl, te[i], 0, 0)),
            _const_spec((None, None, f, d), lambda i, te, nu, nv: (jl, te[i], 0, 0)),
        ],
        out_specs=pl.BlockSpec(memory_space=pl.ANY),
        scratch_shapes=[
            pltpu.SMEM((2, 2 * tm), jnp.int32),
            pltpu.VMEM((2, tm * SUBLANES, LANES), F32),
            pltpu.VMEM((tm, d), BF16),
            pltpu.VMEM((tm, f), BF16),
            pltpu.VMEM((2, tm * SUBLANES, LANES), F32),
            pltpu.SemaphoreType.DMA((2,)), pltpu.SemaphoreType.DMA((2,)),
            pltpu.SemaphoreType.DMA((2,)),
        ],
    )
    return pl.pallas_call(
        functools.partial(_ffn_moe_kernel, tm, FFN_COLS_MOE),
        grid_spec=grid_spec,
        out_shape=jax.ShapeDtypeStruct((n_out_rows, LANES), F32),
        compiler_params=pltpu.CompilerParams(
            dimension_semantics=("arbitrary",), vmem_limit_bytes=VMEM_LIMIT_BYTES,
            disable_bounds_checks=True, has_side_effects=True),
        name="ffn_moe",
    )(tile_expert, n_used, n_valid, idx, h2t, w1, w3, w2)


def _combine_kernel(tc, d, final, x_ref, y0_ref, y1_ref, tw_ref, mod_ref, fg_ref, o_ref):
    g2 = mod_ref[5:6, :]
    tw = tw_ref[...]
    parts = []
    for s8 in range(d // LANES):
        y0 = y0_ref[pl.ds(s8, tc, stride=SUBLANES), :]
        y1 = y1_ref[pl.ds(s8, tc, stride=SUBLANES), :]
        parts.append(tw[:, 0:1] * y0 + tw[:, 1:2] * y1)
    xn = x_ref[...] + g2 * jnp.concatenate(parts, axis=1)
    if final:
        xn = _rms(xn) * fg_ref[...]
    o_ref[...] = xn


def _combine(l, final, x, y2, tw_t, mod, final_g):
    t, d = x.shape
    s = t // mod.shape[1]
    tc = min(COMBINE_ROWS, s)
    nt = t // tc
    return pl.pallas_call(
        functools.partial(_combine_kernel, tc, d, final),
        grid=(nt,),
        in_specs=[
            pl.BlockSpec((tc, d), lambda i: (i, 0)),
            pl.BlockSpec((tc * SUBLANES, LANES), lambda i: (i, 0)),
            pl.BlockSpec((tc * SUBLANES, LANES), lambda i: (nt + i, 0)),
            pl.BlockSpec((tc, 2), lambda i: (i, 0)),
            pl.BlockSpec((None, None, 6, d), lambda i: (l, (i * tc) // s, 0, 0)),
            pl.BlockSpec((1, d), lambda i: (0, 0)),
        ],
        out_specs=pl.BlockSpec((tc, d), lambda i: (i, 0)),
        out_shape=jax.ShapeDtypeStruct((t, d), F32),
        compiler_params=pltpu.CompilerParams(
            dimension_semantics=("arbitrary",), vmem_limit_bytes=VMEM_LIMIT_BYTES),
        name="moe_combine",
    )(x, y2, y2, tw_t, mod, final_g.reshape(1, d))


def _dispatch_plan(ti, tm, nt):
    t = ti.shape[1]
    e_flat = ti.reshape(-1)
    experts = jnp.arange(N_EXPERTS, dtype=jnp.int32)
    counts = jnp.sum((e_flat[:, None] == experts[None, :]).astype(jnp.int32), axis=0)
    order = jnp.argsort(e_flat, stable=True).astype(jnp.int32)
    group_start = jnp.cumsum(counts) - counts
    tiles_e = (counts + tm - 1) // tm
    tile_end = jnp.cumsum(tiles_e)
    tile_start = tile_end - tiles_e
    n_used = tile_end[-1]
    tile_id = jnp.minimum(jnp.arange(nt, dtype=jnp.int32), n_used - 1)
    tile_expert = jnp.minimum(
        jnp.searchsorted(tile_end, tile_id, side="right").astype(jnp.int32), N_EXPERTS - 1)
    first_row = (jnp.arange(nt, dtype=jnp.int32) - tile_start[tile_expert]) * tm
    n_valid = jnp.clip(counts[tile_expert] - first_row, 0, tm)
    n_valid = jnp.where(jnp.arange(nt) < n_used, n_valid, 0).astype(jnp.int32)
    p = first_row[:, None] + jnp.arange(tm, dtype=jnp.int32)[None, :]
    valid = jnp.arange(tm, dtype=jnp.int32)[None, :] < n_valid[:, None]
    a = order[jnp.clip(group_start[tile_expert][:, None] + p, 0, 2 * t - 1)]
    src_row = jnp.where(valid, a % t, 0) * SUBLANES
    dst_row = jnp.where(valid, a, 0) * SUBLANES
    idx = jnp.concatenate([src_row, dst_row], axis=1).astype(jnp.int32)
    return idx, tile_expert, n_used.reshape(1).astype(jnp.int32), n_valid


def kernel(x, c, ada_w, ada_b, w_in, gm_ln_g, gm_ln_b, w_sp, b_sp, w_gm_out, conv_w, conv_b,
           cv_ln_g, cv_ln_b, w_cv_out, w_o, ffn_w1, ffn_w3, ffn_w2, router_w, exp_w1, exp_w3,
           exp_w2, final_g):
    bsz, s, d = x.shape
    depth = ada_w.shape[0]
    t = bsz * s
    assert s % CHUNK == 0 and d % GM_GROUPS == 0 and depth % 2 == 0

    mod = _ada_modulation(c, ada_w, ada_b).reshape(depth, bsz, 6, d)

    win = w_in.astype(BF16)
    wgm, wcv, wo = w_gm_out.astype(BF16), w_cv_out.astype(BF16), w_o.astype(BF16)
    bsp_full = jnp.repeat(jnp.swapaxes(b_sp, 1, 2), d // GM_GROUPS, axis=2)
    zeros = jnp.zeros_like(conv_b)
    vecs = jnp.stack([gm_ln_g, gm_ln_b, cv_ln_g, cv_ln_b, conv_b, zeros, zeros, zeros], axis=1)
    cw = jnp.pad(conv_w, ((0, 0), (0, HALO - CONV_K), (0, 0)))
    cw = jnp.repeat(cw, SUBLANES, axis=1)
    rw_t = jnp.swapaxes(router_w, 1, 2)
    fw1, fw3, fw2 = ffn_w1.astype(BF16), ffn_w3.astype(BF16), ffn_w2.astype(BF16)
    ew1, ew3, ew2 = exp_w1.astype(BF16), exp_w3.astype(BF16), exp_w2.astype(BF16)

    tm = min(FFN_ROWS, t)
    nt = (2 * t) // tm + N_EXPERTS

    for l in range(depth):
        moe = l % 2 == 1
        outs = _mixer(l, moe, x, mod, win, w_sp, bsp_full, vecs, cw, wgm, wcv, wo, rw_t)
        if not moe:
            xn, h2 = outs
            x2 = _ffn_dense(l, xn.reshape(t, d), h2.reshape(t, d), mod, fw1, fw3, fw2)
        else:
            xn, h2t, ti, tw = outs
            idx, tile_expert, n_used, n_valid = _dispatch_plan(ti, tm, nt)
            y2 = _ffn_moe(l // 2, h2t, idx, tile_expert, n_used, n_valid, ew1, ew3, ew2, tm, nt,
                          2 * t * SUBLANES)
            x2 = _combine(l, l == depth - 1, xn.reshape(t, d), y2, tw.T, mod, final_g)
        x = x2.reshape(bsz, s, d)
    return x
```

```python
import functools

import jax
import jax.numpy as jnp
from jax import lax
from jax.experimental import pallas as pl
from jax.experimental.pallas import tpu as pltpu

F32 = jnp.float32
BF16 = jnp.bfloat16

EPS = 1e-6
CHUNK = 128
GM_GROUPS = 8
CONV_K = 31
N_EXPERTS = 8
HALO = 32
CONV_ROWS = 64
CONV_LANES = 128
SUBLANES = 8
LANES = 128
VMEM_LIMIT_BYTES = 56 * 1024 * 1024

MIX_ROWS = 512
FFN_ROWS = 512
FFN_COLS_DENSE = 256
FFN_COLS_MOE = 512
COMBINE_ROWS = 256


def _const_spec(shape, index_map):
    return pl.BlockSpec(shape, index_map, pipeline_mode=pl.Buffered(1))


def _rms(x):
    return x * lax.rsqrt(jnp.mean(x * x, axis=-1, keepdims=True) + EPS)


def _layer_norm(x, g, b):
    mu = jnp.mean(x, axis=-1, keepdims=True)
    d = x - mu
    var = jnp.mean(d * d, axis=-1, keepdims=True)
    return d * lax.rsqrt(var + EPS) * g + b


def _ada_kernel(c_ref, w_ref, b_ref, o_ref):
    sc = jax.nn.silu(c_ref[...])
    o_ref[...] = jnp.dot(sc, w_ref[...], precision=lax.Precision.HIGHEST,
                         preferred_element_type=F32) + b_ref[...]


def _ada_modulation(c, ada_w, ada_b):
    depth, d, d6 = ada_w.shape
    bsz = c.shape[0]
    nj = d6 // d
    return pl.pallas_call(
        _ada_kernel,
        grid=(depth, nj),
        in_specs=[
            pl.BlockSpec((bsz, d), lambda l, j: (0, 0)),
            pl.BlockSpec((None, d, d), lambda l, j: (l, 0, j)),
            pl.BlockSpec((None, 1, d), lambda l, j: (l, 0, j)),
        ],
        out_specs=pl.BlockSpec((None, bsz, d), lambda l, j: (l, 0, j)),
        out_shape=jax.ShapeDtypeStruct((depth, bsz, d6), F32),
        compiler_params=pltpu.CompilerParams(
            dimension_semantics=("arbitrary", "arbitrary"), vmem_limit_bytes=VMEM_LIMIT_BYTES),
        name="ada_modulation",
    )(c, ada_w, ada_b.reshape(depth, 1, d6))


class _Interleaver:
    def __init__(self):
        self.items = []

    def add(self, name, unit, cost, deps, fn):
        self.items.append((name, unit, cost, tuple(deps), fn))

    def emit(self):
        done, unit_time, pending = {}, {"mxu": 0, "vpu": 0}, list(self.items)
        while pending:
            best = None
            for it in pending:
                if all(dep in done for dep in it[3]):
                    start = max([unit_time[it[1]]] + [done[dep] for dep in it[3]])
                    if best is None or start < best[0]:
                        best = (start, it)
            start, it = best
            it[4]()
            done[it[0]] = unit_time[it[1]] = start + it[2]
            pending.remove(it)


def _mixer_kernel(moe, ts, d, x_ref, mod_ref, win_ref, wsp_ref, bsp_ref, vec_ref, cwb_ref,
                  wgm_ref, wcv_ref, wo_ref, *rest):
    if moe:
        rw_ref, xo_ref, h2_ref, ti_ref, tw_ref = rest[:5]
        scratch = rest[5:]
    else:
        xo_ref, h2_ref = rest[:2]
        scratch = rest[2:]
    (h_buf, hp32, hp_buf, m_buf, z0, z1, z2, z3, zcp, u_buf, v_buf, e_buf, tail_buf, ga_buf, gb_buf,
     ya_buf, yb_buf, wspm) = scratch[:18]
    h2f = scratch[18] if moe else None
    seg = ts // SUBLANES
    assert seg % rb_rows(seg) == 0 and seg >= HALO
    rb = rb_rows(seg)
    qw = 256
    nq = d // qw
    row_blocks = list(range(0, ts, rb))
    first_tile = jnp.logical_and(pl.program_id(0) == 0, pl.program_id(1) == 0)

    @pl.when(first_tile)
    def _():
        row = lax.broadcasted_iota(jnp.int32, (CHUNK, CHUNK), 0)
        col = lax.broadcasted_iota(jnp.int32, (CHUNK, CHUNK), 1)
        for g in range(GM_GROUPS):
            wspm[g] = jnp.where(row >= col, wsp_ref[g], 0.0).astype(BF16)

    @pl.when(pl.program_id(1) == 0)
    def _():
        tail_buf[...] = jnp.zeros(tail_buf.shape, F32)

    sh1, sc1, g1 = mod_ref[0:1, :], mod_ref[1:2, :], mod_ref[2:3, :]
    sh2, sc2 = mod_ref[3:4, :], mod_ref[4:5, :]
    gm_g, gm_b = vec_ref[0:1, :], vec_ref[1:2, :]
    cv_g, cv_b, conv_b = vec_ref[2:3, :], vec_ref[3:4, :], vec_ref[4:5, :]
    gd = d // GM_GROUPS
    off0 = HALO - CONV_K + 1
    conv_groups = CONV_ROWS // SUBLANES
    sched = _Interleaver()

    def qcols(q):
        return slice(q * qw, (q + 1) * qw)

    def perm_rows(r):
        return pl.ds((r % seg) * SUBLANES + r // seg, rb, stride=SUBLANES)

    def norm_item(r):
        def fn():
            xb = x_ref[r:r + rb, :]
            h = _rms(xb) * (1.0 + sc1) + sh1
            h_buf[r:r + rb, :] = h.astype(BF16)
            for s8 in range(d // LANES):
                hp32[s8, perm_rows(r), :] = h[:, s8 * LANES:(s8 + 1) * LANES]
        return fn
    for r in row_blocks:
        sched.add(("norm", r), "vpu", 90, [], norm_item(r))
    all_norm = [("norm", r) for r in row_blocks]

    def perm_pack_item(p0):
        def fn():
            for s8 in range(d // LANES):
                hp_buf[p0:p0 + rb, s8 * LANES:(s8 + 1) * LANES] = hp32[s8, p0:p0 + rb, :].astype(BF16)
        return fn
    for p0 in row_blocks:
        sched.add(("pack", p0), "vpu", 20, all_norm, perm_pack_item(p0))
    all_pack = [("pack", p0) for p0 in row_blocks]

    def in_dot_item(split, q, dst, lhs=h_buf):
        def fn():
            c0 = split * d + q * qw
            dst[:, qcols(q)] = jnp.dot(lhs[...], win_ref[:, c0:c0 + qw], preferred_element_type=F32)
        return fn

    def a_item(q):
        def fn():
            e_buf[HALO * SUBLANES:, qcols(q)] = z0[:, qcols(q)] * jax.nn.sigmoid(z1[:, qcols(q)])
        return fn

    def halo_item(q):
        def fn():
            shape3 = (HALO, SUBLANES, qw)
            cur = e_buf[seg * SUBLANES:, qcols(q)]
            prev = tail_buf[:, qcols(q)]
            first = lax.broadcasted_iota(jnp.int32, shape3, 1) == 0
            halo = jnp.where(first, pltpu.roll(prev.reshape(shape3), 1, axis=1),
                             pltpu.roll(cur.reshape(shape3), 1, axis=1))
            e_buf[0:HALO * SUBLANES, qcols(q)] = halo.reshape(HALO * SUBLANES, qw)
            tail_buf[:, qcols(q)] = cur
        return fn

    def conv_item(jb, q):
        def fn():
            for c0 in range(q * qw, (q + 1) * qw, CONV_LANES):
                cols = slice(c0, c0 + CONV_LANES)
                g0 = jb + off0
                win = e_buf[g0 * SUBLANES:(g0 + conv_groups + CONV_K - 1) * SUBLANES, cols]
                win = win.reshape(conv_groups + CONV_K - 1, SUBLANES, CONV_LANES)
                acc = jnp.broadcast_to(conv_b[:, cols], (conv_groups, SUBLANES, CONV_LANES))
                for k in range(CONV_K):
                    w_k = cwb_ref[k * SUBLANES:(k + 1) * SUBLANES, cols]
                    acc = acc + w_k[None] * win[k:k + conv_groups]
                zcp[c0 // LANES, jb * SUBLANES:(jb + conv_groups) * SUBLANES, :] = (
                    acc.reshape(CONV_ROWS, CONV_LANES))
        return fn

    def cln_item(r):
        def fn():
            y = jnp.concatenate([zcp[s8, perm_rows(r), :] for s8 in range(d // LANES)], axis=1)
            yb_buf[r:r + rb, :] = jax.nn.silu(_layer_norm(y, cv_g, cv_b)).astype(BF16)
        return fn

    def u_item(q):
        def fn():
            u_buf[:, qcols(q)] = jax.nn.gelu(z2[:, qcols(q)]).astype(BF16)
        return fn

    def v_item(r):
        def fn():
            v = jax.nn.gelu(z3[r:r + rb, :])
            v_buf[r:r + rb, :] = _layer_norm(v, gm_g, gm_b).astype(BF16)
        return fn

    def gate_item(q):
        def fn():
            ga_buf[:, qcols(q)] = jax.nn.sigmoid(z0[:, qcols(q)]).astype(BF16)
            gb_buf[:, qcols(q)] = jax.nn.sigmoid(z1[:, qcols(q)]).astype(BF16)
        return fn

    def spatial_item(ci):
        def fn():
            rows = slice(ci * CHUNK, (ci + 1) * CHUNK)
            for g in range(GM_GROUPS):
                cols = slice(g * gd, (g + 1) * gd)
                sv = jnp.dot(wspm[g], v_buf[rows, cols], preferred_element_type=F32) + bsp_ref[:, cols]
                ya_buf[rows, cols] = (u_buf[rows, cols].astype(F32) * sv).astype(BF16)
        return fn

    def out_dot_item(src, w_ref, dst, q, rows=slice(None)):
        def fn():
            dst[rows, qcols(q)] = jnp.dot(src[rows, :], w_ref[:, qcols(q)], preferred_element_type=F32)
        return fn

    def merge_item(q):
        def fn():
            m = (ga_buf[:, qcols(q)].astype(F32) * z2[:, qcols(q)]
                 + gb_buf[:, qcols(q)].astype(F32) * z3[:, qcols(q)])
            m_buf[:, qcols(q)] = m.astype(BF16)
        return fn

    def out_item(r):
        def fn():
            xn = x_ref[r:r + rb, :] + g1 * z0[r:r + rb, :]
            xo_ref[r:r + rb, :] = xn
            h2 = _rms(xn) * (1.0 + sc2) + sh2
            if moe:
                h2f[r:r + rb, :] = h2
                for s8 in range(d // LANES):
                    h2_ref[pl.ds(r * SUBLANES + s8, rb, stride=SUBLANES), :] = (
                        h2[:, s8 * LANES:(s8 + 1) * LANES])
            else:
                h2_ref[r:r + rb, :] = h2.astype(h2_ref.dtype)
        return fn

    conv_blocks = list(range(0, seg, conv_groups))
    chunks = list(range(ts // CHUNK))
    halves = [slice(0, ts // 2), slice(ts // 2, ts)]
    for q in range(nq):
        sched.add(("d_cva", q), "mxu", 512, all_pack, in_dot_item(2, q, z0, hp_buf))
        sched.add(("d_cvg", q), "mxu", 512, all_pack, in_dot_item(3, q, z1, hp_buf))
        sched.add(("a", q), "vpu", 300, [("d_cva", q), ("d_cvg", q)], a_item(q))
        sched.add(("halo", q), "vpu", 60, [("a", q)], halo_item(q))
    for q in range(nq):
        sched.add(("d_u", q), "mxu", 512, all_norm, in_dot_item(0, q, z2))
        sched.add(("u", q), "vpu", 350, [("d_u", q)], u_item(q))
    for q in range(nq):
        sched.add(("d_v", q), "mxu", 512, all_norm, in_dot_item(1, q, z3))
    for r in row_blocks:
        sched.add(("v", r), "vpu", 170, [("d_v", q) for q in range(nq)], v_item(r))
    for q in range(nq):
        sched.add(("d_ga", q), "mxu", 512, all_norm + [("a", q)], in_dot_item(4, q, z0))
        sched.add(("d_gb", q), "mxu", 512, all_norm + [("a", q)], in_dot_item(5, q, z1))
        sched.add(("g", q), "vpu", 550, [("d_ga", q), ("d_gb", q)], gate_item(q))
    for ci in chunks:
        deps = [("v", r) for r in row_blocks if r // CHUNK == ci] + [("u", q) for q in range(nq)]
        sched.add(("sp", ci), "mxu", 600, deps, spatial_item(ci))
    for q in range(nq):
        for jb in conv_blocks:
            deps = [("a", q)] + ([("halo", q)] if jb + off0 < HALO else [])
            sched.add(("conv", jb, q), "vpu", 260, deps, conv_item(jb, q))
    for r in row_blocks:
        j0 = r % seg
        deps = [("conv", jb, q) for jb in conv_blocks if j0 <= jb < j0 + rb for q in range(nq)]
        sched.add(("cln", r), "vpu", 180, deps, cln_item(r))
    all_sp = [("sp", ci) for ci in chunks]
    all_cln = [("cln", r) for r in row_blocks]
    all_v = [("v", r) for r in row_blocks]
    for q in range(nq):
        sched.add(("d_ya", q), "mxu", 512, all_sp, out_dot_item(ya_buf, wgm_ref, z2, q))
        sched.add(("d_yb", q), "mxu", 512, all_cln + all_v, out_dot_item(yb_buf, wcv_ref, z3, q))
        sched.add(("m", q), "vpu", 250, [("d_ya", q), ("d_yb", q), ("g", q)], merge_item(q))
    all_m = [("m", q) for q in range(nq)]
    for hi, rows in enumerate(halves):
        for q in range(nq):
            sched.add(("d_o", hi, q), "mxu", 256, all_m, out_dot_item(m_buf, wo_ref, z0, q, rows))
        for r in row_blocks:
            if rows.start <= r < rows.stop:
                sched.add(("out", r), "vpu", 100, [("d_o", hi, q) for q in range(nq)], out_item(r))
    sched.emit()

    if moe:
        nt_dims = (((1,), (1,)), ((), ()))
        hf = h2f[...]
        hi = hf.astype(BF16)
        lo = (hf - hi.astype(F32)).astype(BF16)
        part = (lax.dot_general(rw_ref[...], hi, nt_dims, preferred_element_type=F32)
                + lax.dot_general(rw_ref[...], lo, nt_dims, preferred_element_type=F32))
        lg = part[0:N_EXPERTS, :] + part[N_EXPERTS:2 * N_EXPERTS, :]
        ls = [lg[e:e + 1, :] for e in range(N_EXPERTS)]
        m1 = functools.reduce(jnp.maximum, ls)
        i1 = jnp.full(m1.shape, N_EXPERTS - 1, jnp.int32)
        for e in range(N_EXPERTS - 2, -1, -1):
            i1 = jnp.where(ls[e] == m1, e, i1)
        ls2 = [jnp.where(i1 == e, -jnp.inf, ls[e]) for e in range(N_EXPERTS)]
        m2 = functools.reduce(jnp.maximum, ls2)
        i2 = jnp.full(m1.shape, N_EXPERTS - 1, jnp.int32)
        for e in range(N_EXPERTS - 2, -1, -1):
            i2 = jnp.where(ls2[e] == m2, e, i2)
        e2 = jnp.exp(m2 - m1)
        den = 1.0 + e2
        ti_ref[0:1, :] = i1
        ti_ref[1:2, :] = i2
        tw_ref[0:1, :] = 1.0 / den
        tw_ref[1:2, :] = e2 / den


def rb_rows(seg):
    return min(32, seg)


def _mixer(l, moe, x, mod, win, wsp, bsp_full, vecs, cw, wgm, wcv, wo, rw_t):
    bsz, s, d = x.shape
    ts = min(MIX_ROWS, s)
    ns = s // ts
    t = bsz * s
    in_specs = [
        pl.BlockSpec((None, ts, d), lambda b, i: (b, i, 0)),
        pl.BlockSpec((None, None, 6, d), lambda b, i: (l, b, 0, 0)),
        _const_spec((None, d, 6 * d), lambda b, i: (l, 0, 0)),
        _const_spec((None, GM_GROUPS, CHUNK, CHUNK), lambda b, i: (l, 0, 0, 0)),
        _const_spec((None, CHUNK, d), lambda b, i: (l, 0, 0)),
        _const_spec((None, 8, d), lambda b, i: (l, 0, 0)),
        _const_spec((None, HALO * SUBLANES, d), lambda b, i: (l, 0, 0)),
        _const_spec((None, d, d), lambda b, i: (l, 0, 0)),
        _const_spec((None, d, d), lambda b, i: (l, 0, 0)),
        _const_spec((None, d, d), lambda b, i: (l, 0, 0)),
    ]
    args = [x, mod, win, wsp, bsp_full, vecs, cw, wgm, wcv, wo]
    tok_rows = d // LANES
    assert tok_rows == SUBLANES
    out_specs = [pl.BlockSpec((None, ts, d), lambda b, i: (b, i, 0))]
    out_shape = [jax.ShapeDtypeStruct((bsz, s, d), F32)]
    if moe:
        out_specs.append(pl.BlockSpec((ts * tok_rows, LANES), lambda b, i: (b * ns + i, 0)))
        out_shape.append(jax.ShapeDtypeStruct((t * tok_rows, LANES), F32))
    else:
        out_specs.append(pl.BlockSpec((None, ts, d), lambda b, i: (b, i, 0)))
        out_shape.append(jax.ShapeDtypeStruct((bsz, s, d), BF16))
    if moe:
        in_specs.append(_const_spec((None, 2 * N_EXPERTS, d), lambda b, i: (l // 2, 0, 0)))
        args.append(rw_t)
        out_specs += [pl.BlockSpec((2, ts), lambda b, i: (0, b * ns + i)),
                      pl.BlockSpec((2, ts), lambda b, i: (0, b * ns + i))]
        out_shape += [jax.ShapeDtypeStruct((2, t), jnp.int32), jax.ShapeDtypeStruct((2, t), F32)]
    scratch = [
        pltpu.VMEM((ts, d), BF16),
        pltpu.VMEM((d // LANES, ts, LANES), F32),
        pltpu.VMEM((ts, d), BF16),
        pltpu.VMEM((ts, d), BF16),
        pltpu.VMEM((ts, d), F32),
        pltpu.VMEM((ts, d), F32),
        pltpu.VMEM((ts, d), F32),
        pltpu.VMEM((ts, d), F32),
        pltpu.VMEM((d // LANES, ts, LANES), F32),
        pltpu.VMEM((ts, d), BF16),
        pltpu.VMEM((ts, d), BF16),
        pltpu.VMEM((ts + HALO * SUBLANES, d), F32),
        pltpu.VMEM((HALO * SUBLANES, d), F32),
        pltpu.VMEM((ts, d), BF16),
        pltpu.VMEM((ts, d), BF16),
        pltpu.VMEM((ts, d), BF16),
        pltpu.VMEM((ts, d), BF16),
        pltpu.VMEM((GM_GROUPS, CHUNK, CHUNK), BF16),
    ]
    if moe:
        scratch.append(pltpu.VMEM((ts, d), F32))
    return pl.pallas_call(
        functools.partial(_mixer_kernel, moe, ts, d),
        grid=(bsz, ns),
        in_specs=in_specs,
        out_specs=out_specs,
        out_shape=out_shape,
        scratch_shapes=scratch,
        compiler_params=pltpu.CompilerParams(
            dimension_semantics=("arbitrary", "arbitrary"), vmem_limit_bytes=VMEM_LIMIT_BYTES),
        name="mixer_moe" if moe else "mixer_dense",
    )(*args)


def _swiglu_tile(x_ref, w1_ref, w3_ref, w2_ref, g_buf, fc):
    f = w1_ref.shape[-1]
    x = x_ref[...]
    for c0 in range(0, f, fc):
        cols = slice(c0, c0 + fc)
        h1 = jnp.dot(x, w1_ref[:, cols], preferred_element_type=F32)
        h3 = jnp.dot(x, w3_ref[:, cols], preferred_element_type=F32)
        g_buf[:, cols] = (jax.nn.silu(h1) * h3).astype(BF16)
    return jnp.dot(g_buf[...], w2_ref[...], preferred_element_type=F32)


def _ffn_dense_kernel(fc, x_ref, h_ref, mod_ref, w1_ref, w3_ref, w2_ref, o_ref, g_buf):
    g2 = mod_ref[5:6, :]
    o_ref[...] = x_ref[...] + g2 * _swiglu_tile(h_ref, w1_ref, w3_ref, w2_ref, g_buf, fc)


def _ffn_dense(l, x, h2, mod, w1, w3, w2):
    t, d = x.shape
    f = w1.shape[-1]
    s = t // mod.shape[1]
    tm = min(FFN_ROWS, s)
    j = l // 2
    return pl.pallas_call(
        functools.partial(_ffn_dense_kernel, FFN_COLS_DENSE),
        grid=(t // tm,),
        in_specs=[
            pl.BlockSpec((tm, d), lambda i: (i, 0)),
            pl.BlockSpec((tm, d), lambda i: (i, 0)),
            pl.BlockSpec((None, None, 6, d), lambda i: (l, (i * tm) // s, 0, 0)),
            _const_spec((None, d, f), lambda i: (j, 0, 0)),
            _const_spec((None, d, f), lambda i: (j, 0, 0)),
            _const_spec((None, f, d), lambda i: (j, 0, 0)),
        ],
        out_specs=pl.BlockSpec((tm, d), lambda i: (i, 0)),
        out_shape=jax.ShapeDtypeStruct((t, d), F32),
        scratch_shapes=[pltpu.VMEM((tm, f), BF16)],
        compiler_params=pltpu.CompilerParams(
            dimension_semantics=("arbitrary",), vmem_limit_bytes=VMEM_LIMIT_BYTES),
        name="ffn_dense",
    )(x, h2, mod, w1, w3, w2)


def _token_copy(src, src_row, dst, dst_row, sem):
    return pltpu.make_async_copy(src.at[pl.ds(pl.multiple_of(src_row, SUBLANES), SUBLANES), :],
                                 dst.at[pl.ds(pl.multiple_of(dst_row, SUBLANES), SUBLANES), :], sem)


def _ffn_moe_kernel(tm, fc, te_ref, nu_ref, nv_ref, idx_hbm, h_hbm, w1_ref, w3_ref, w2_ref, y_hbm,
                    idx_smem, xbuf, xbf, g_buf, ybuf, idx_sem, in_sem, out_sem):
    i = pl.program_id(0)
    n_used = nu_ref[0]
    slot = lax.rem(i, 2)
    d = xbf.shape[1]

    def idx_base(sl):
        return pl.multiple_of(sl * (2 * tm), 2 * tm)

    def idx_copy(tile, sl):
        return pltpu.make_async_copy(idx_hbm.at[tile], idx_smem.at[pl.ds(idx_base(sl), 2 * tm)],
                                     idx_sem.at[sl])

    def start_gathers(sl):
        base = idx_base(sl)
        def body(r, carry):
            _token_copy(h_hbm, idx_smem[base + r], xbuf.at[sl], r * SUBLANES, in_sem.at[sl]).start()
            return carry
        lax.fori_loop(0, tm, body, 0, unroll=8)

    def wait_gathers(sl):
        pltpu.make_async_copy(h_hbm.at[pl.ds(0, tm * SUBLANES), :], xbuf.at[sl], in_sem.at[sl]).wait()

    def scatter_copy(sl, r):
        return _token_copy(ybuf.at[sl], r * SUBLANES, y_hbm, idx_smem[idx_base(sl) + tm + r],
                           out_sem.at[sl])

    def start_scatters(sl, n_valid):
        @pl.when(n_valid == tm)
        def _():
            def body(r, carry):
                scatter_copy(sl, r).start()
                return carry
            lax.fori_loop(0, tm, body, 0, unroll=8)
        @pl.when(n_valid < tm)
        def _():
            def body(r, carry):
                scatter_copy(sl, r).start()
                return carry
            lax.fori_loop(0, n_valid, body, 0)

    def wait_scatters(sl, n_valid):
        @pl.when(n_valid == tm)
        def _():
            pltpu.make_async_copy(ybuf.at[sl], y_hbm.at[pl.ds(0, tm * SUBLANES), :], out_sem.at[sl]).wait()
        @pl.when(n_valid < tm)
        def _():
            def body(r, carry):
                _token_copy(ybuf.at[sl], r * SUBLANES, y_hbm, 0, out_sem.at[sl]).wait()
                return carry
            lax.fori_loop(0, n_valid, body, 0)

    @pl.when(i == 0)
    def _():
        idx_copy(0, 0).start()
        idx_copy(0, 0).wait()
        start_gathers(0)
        @pl.when(n_used > 1)
        def _():
            idx_copy(1, 1).start()

    @pl.when(i < n_used)
    def _():
        wait_gathers(slot)
        @pl.when(i + 1 < n_used)
        def _():
            idx_copy(i + 1, 1 - slot).wait()
            start_gathers(1 - slot)
        for s8 in range(d // LANES):
            xbf[:, s8 * LANES:(s8 + 1) * LANES] = (
                xbuf[slot, pl.ds(s8, tm, stride=SUBLANES), :].astype(BF16))
        y = _swiglu_tile(xbf, w1_ref, w3_ref, w2_ref, g_buf, fc)
        @pl.when(i >= 2)
        def _():
            wait_scatters(slot, nv_ref[jnp.maximum(i - 2, 0)])
        for s8 in range(d // LANES):
            ybuf[slot, pl.ds(s8, tm, stride=SUBLANES), :] = y[:, s8 * LANES:(s8 + 1) * LANES]
        start_scatters(slot, nv_ref[i])
        @pl.when(i + 2 < n_used)
        def _():
            idx_copy(i + 2, slot).start()
        @pl.when(i == n_used - 1)
        def _():
            @pl.when(i >= 1)
            def _():
                wait_scatters(1 - slot, nv_ref[jnp.maximum(i - 1, 0)])
            wait_scatters(slot, nv_ref[i])


def _ffn_moe(jl, h2t, idx, tile_expert, n_used, n_valid, w1, w3, w2, tm, nt, n_out_rows):
    d, f = w1.shape[-2:]
    grid_spec = pltpu.PrefetchScalarGridSpec(
        num_scalar_prefetch=3,
        grid=(nt,),
        in_specs=[
            pl.BlockSpec(memory_space=pl.ANY),
            pl.BlockSpec(memory_space=pl.ANY),
            _const_spec((None, None, d, f), lambda i, te, nu, nv: (jl, te[i], 0, 0)),
            _const_spec((None, None, d, f), lambda i, te, nu, nv: (jl, te[i], 0, 0)),
            _const_spec((None, None, f, d), lambda i, te, nu, nv: (jl, te[i], 0, 0)),
        ],
        out_specs=pl.BlockSpec(memory_space=pl.ANY),
        scratch_shapes=[
            pltpu.SMEM((4 * tm,), jnp.int32),
            pltpu.VMEM((2, tm * SUBLANES, LANES), F32),
            pltpu.VMEM((tm, d), BF16),
            pltpu.VMEM((tm, f), BF16),
            pltpu.VMEM((2, tm * SUBLANES, LANES), F32),
            pltpu.SemaphoreType.DMA((2,)), pltpu.SemaphoreType.DMA((2,)),
            pltpu.SemaphoreType.DMA((2,)),
        ],
    )
    return pl.pallas_call(
        functools.partial(_ffn_moe_kernel, tm, FFN_COLS_MOE),
        grid_spec=grid_spec,
        out_shape=jax.ShapeDtypeStruct((n_out_rows, LANES), F32),
        compiler_params=pltpu.CompilerParams(
            dimension_semantics=("arbitrary",), vmem_limit_bytes=VMEM_LIMIT_BYTES,
            disable_bounds_checks=True, has_side_effects=True),
        name="ffn_moe",
    )(tile_expert, n_used, n_valid, idx, h2t, w1, w3, w2)


def _combine_kernel(tc, d, final, x_ref, y0_ref, y1_ref, tw_ref, mod_ref, fg_ref, o_ref):
    g2 = mod_ref[5:6, :]
    tw = tw_ref[...]
    parts = []
    for s8 in range(d // LANES):
        y0 = y0_ref[pl.ds(s8, tc, stride=SUBLANES), :]
        y1 = y1_ref[pl.ds(s8, tc, stride=SUBLANES), :]
        parts.append(tw[:, 0:1] * y0 + tw[:, 1:2] * y1)
    xn = x_ref[...] + g2 * jnp.concatenate(parts, axis=1)
    if final:
        xn = _rms(xn) * fg_ref[...]
    o_ref[...] = xn


def _combine(l, final, x, y2, tw_t, mod, final_g):
    t, d = x.shape
    s = t // mod.shape[1]
    tc = min(COMBINE_ROWS, s)
    nt = t // tc
    return pl.pallas_call(
        functools.partial(_combine_kernel, tc, d, final),
        grid=(nt,),
        in_specs=[
            pl.BlockSpec((tc, d), lambda i: (i, 0)),
            pl.BlockSpec((tc * SUBLANES, LANES), lambda i: (i, 0)),
            pl.BlockSpec((tc * SUBLANES, LANES), lambda i: (nt + i, 0)),
            pl.BlockSpec((tc, 2), lambda i: (i, 0)),
            pl.BlockSpec((None, None, 6, d), lambda i: (l, (i * tc) // s, 0, 0)),
            pl.BlockSpec((1, d), lambda i: (0, 0)),
        ],
        out_specs=pl.BlockSpec((tc, d), lambda i: (i, 0)),
        out_shape=jax.ShapeDtypeStruct((t, d), F32),
        compiler_params=pltpu.CompilerParams(
            dimension_semantics=("arbitrary",), vmem_limit_bytes=VMEM_LIMIT_BYTES),
        name="moe_combine",
    )(x, y2, y2, tw_t, mod, final_g.reshape(1, d))


def _dispatch_plan(ti, tm, nt):
    t = ti.shape[1]
    e_flat = ti.reshape(-1)
    experts = jnp.arange(N_EXPERTS, dtype=jnp.int32)
    counts = jnp.sum((e_flat[:, None] == experts[None, :]).astype(jnp.int32), axis=0)
    order = jnp.argsort(e_flat, stable=True).astype(jnp.int32)
    group_start = jnp.cumsum(counts) - counts
    tiles_e = (counts + tm - 1) // tm
    tile_end = jnp.cumsum(tiles_e)
    tile_start = tile_end - tiles_e
    n_used = tile_end[-1]
    tile_id = jnp.minimum(jnp.arange(nt, dtype=jnp.int32), n_used - 1)
    tile_expert = jnp.minimum(
        jnp.searchsorted(tile_end, tile_id, side="right").astype(jnp.int32), N_EXPERTS - 1)
    first_row = (jnp.arange(nt, dtype=jnp.int32) - tile_start[tile_expert]) * tm
    n_valid = jnp.clip(counts[tile_expert] - first_row, 0, tm)
    n_valid = jnp.where(jnp.arange(nt) < n_used, n_valid, 0).astype(jnp.int32)
    p = first_row[:, None] + jnp.arange(tm, dtype=jnp.int32)[None, :]
    valid = jnp.arange(tm, dtype=jnp.int32)[None, :] < n_valid[:, None]
    a = order[jnp.clip(group_start[tile_expert][:, None] + p, 0, 2 * t - 1)]
    src_row = jnp.where(valid, a % t, 0) * SUBLANES
    dst_row = jnp.where(valid, a, 0) * SUBLANES
    idx = jnp.concatenate([src_row, dst_row], axis=1).astype(jnp.int32)
    return idx, tile_expert, n_used.reshape(1).astype(jnp.int32), n_valid


def kernel(x, c, ada_w, ada_b, w_in, gm_ln_g, gm_ln_b, w_sp, b_sp, w_gm_out, conv_w, conv_b,
           cv_ln_g, cv_ln_b, w_cv_out, w_o, ffn_w1, ffn_w3, ffn_w2, router_w, exp_w1, exp_w3,
           exp_w2, final_g):
    bsz, s, d = x.shape
    depth = ada_w.shape[0]
    t = bsz * s
    assert s % CHUNK == 0 and d % GM_GROUPS == 0 and depth % 2 == 0

    mod = _ada_modulation(c, ada_w, ada_b).reshape(depth, bsz, 6, d)

    win = w_in.astype(BF16)
    wgm, wcv, wo = w_gm_out.astype(BF16), w_cv_out.astype(BF16), w_o.astype(BF16)
    bsp_full = jnp.repeat(jnp.swapaxes(b_sp, 1, 2), d // GM_GROUPS, axis=2)
    zeros = jnp.zeros_like(conv_b)
    vecs = jnp.stack([gm_ln_g, gm_ln_b, cv_ln_g, cv_ln_b, conv_b, zeros, zeros, zeros], axis=1)
    cw = jnp.pad(conv_w, ((0, 0), (0, HALO - CONV_K), (0, 0)))
    cw = jnp.repeat(cw, SUBLANES, axis=1)
    rw_t = jnp.swapaxes(router_w, 1, 2)
    rw_hi = rw_t.astype(BF16)
    rw_t = jnp.concatenate([rw_hi, (rw_t - rw_hi.astype(F32)).astype(BF16)], axis=1)
    fw1, fw3, fw2 = ffn_w1.astype(BF16), ffn_w3.astype(BF16), ffn_w2.astype(BF16)
    ew1, ew3, ew2 = exp_w1.astype(BF16), exp_w3.astype(BF16), exp_w2.astype(BF16)

    tm = min(FFN_ROWS, t)
    nt = (2 * t) // tm + N_EXPERTS

    for l in range(depth):
        moe = l % 2 == 1
        outs = _mixer(l, moe, x, mod, win, w_sp, bsp_full, vecs, cw, wgm, wcv, wo, rw_t)
        if not moe:
            xn, h2 = outs
            x2 = _ffn_dense(l, xn.reshape(t, d), h2.reshape(t, d), mod, fw1, fw3, fw2)
        else:
            xn, h2t, ti, tw = outs
            idx, tile_expert, n_used, n_valid = _dispatch_plan(ti, tm, nt)
            y2 = _ffn_moe(l // 2, h2t, idx, tile_expert, n_used, n_valid, ew1, ew3, ew2, tm, nt,
                          2 * t * SUBLANES)
            x2 = _combine(l, l == depth - 1, xn.reshape(t, d), y2, tw.T, mod, final_g)
        x = x2.reshape(bsz, s, d)
    return x
```

```python
import functools

import jax
import jax.numpy as jnp
from jax import lax
from jax.experimental import pallas as pl
from jax.experimental.pallas import tpu as pltpu

F32 = jnp.float32
BF16 = jnp.bfloat16

EPS = 1e-6
CHUNK = 128
GM_GROUPS = 8
CONV_K = 31
N_EXPERTS = 8
HALO = 32
CONV_ROWS = 64
CONV_LANES = 128
SUBLANES = 8
LANES = 128
VMEM_LIMIT_BYTES = 56 * 1024 * 1024

MIX_ROWS = 512
FFN_ROWS = 512
FFN_COLS_DENSE = 256
FFN_COLS_MOE = 512
COMBINE_ROWS = 256
DMA_UNROLL = 8


def _const_spec(shape, index_map):
    return pl.BlockSpec(shape, index_map, pipeline_mode=pl.Buffered(1))


def _rms(x):
    return x * lax.rsqrt(jnp.mean(x * x, axis=-1, keepdims=True) + EPS)


def _layer_norm(x, g, b):
    mu = jnp.mean(x, axis=-1, keepdims=True)
    d = x - mu
    var = jnp.mean(d * d, axis=-1, keepdims=True)
    return d * lax.rsqrt(var + EPS) * g + b


def _ada_kernel(c_ref, w_ref, b_ref, o_ref):
    sc = jax.nn.silu(c_ref[...])
    o_ref[...] = jnp.dot(sc, w_ref[...], precision=lax.Precision.HIGHEST,
                         preferred_element_type=F32) + b_ref[...]


def _ada_modulation(c, ada_w, ada_b):
    depth, d, d6 = ada_w.shape
    bsz = c.shape[0]
    nj = d6 // d
    return pl.pallas_call(
        _ada_kernel,
        grid=(depth, nj),
        in_specs=[
            pl.BlockSpec((bsz, d), lambda l, j: (0, 0)),
            pl.BlockSpec((None, d, d), lambda l, j: (l, 0, j)),
            pl.BlockSpec((None, 1, d), lambda l, j: (l, 0, j)),
        ],
        out_specs=pl.BlockSpec((None, bsz, d), lambda l, j: (l, 0, j)),
        out_shape=jax.ShapeDtypeStruct((depth, bsz, d6), F32),
        compiler_params=pltpu.CompilerParams(
            dimension_semantics=("arbitrary", "arbitrary"), vmem_limit_bytes=VMEM_LIMIT_BYTES),
        name="ada_modulation",
    )(c, ada_w, ada_b.reshape(depth, 1, d6))


class _Interleaver:
    def __init__(self):
        self.items = []

    def add(self, name, unit, cost, deps, fn):
        self.items.append((name, unit, cost, tuple(deps), fn))

    def emit(self):
        done, unit_time, pending = {}, {"mxu": 0, "vpu": 0}, list(self.items)
        while pending:
            best = None
            for it in pending:
                if all(dep in done for dep in it[3]):
                    start = max([unit_time[it[1]]] + [done[dep] for dep in it[3]])
                    if best is None or start < best[0]:
                        best = (start, it)
            start, it = best
            it[4]()
            done[it[0]] = unit_time[it[1]] = start + it[2]
            pending.remove(it)


def _mixer_kernel(moe, ts, d, x_ref, mod_ref, win_ref, wsp_ref, bsp_ref, vec_ref, cwb_ref,
                  wgm_ref, wcv_ref, wo_ref, *rest):
    if moe:
        rw_ref, xo_ref, h2_ref, ti_ref, tw_ref = rest[:5]
        scratch = rest[5:]
    else:
        xo_ref, h2_ref = rest[:2]
        scratch = rest[2:]
    (h_buf, hp32, hp_buf, m_buf, z0, z1, z2, z3, zcp, u_buf, v_buf, e_buf, tail_buf, ga_buf, gb_buf,
     ya_buf, yb_buf, wspm) = scratch[:18]
    h2f = scratch[18] if moe else None
    seg = ts // SUBLANES
    assert seg % rb_rows(seg) == 0 and seg >= HALO
    rb = rb_rows(seg)
    qw = 256
    nq = d // qw
    row_blocks = list(range(0, ts, rb))
    first_tile = jnp.logical_and(pl.program_id(0) == 0, pl.program_id(1) == 0)

    @pl.when(first_tile)
    def _():
        row = lax.broadcasted_iota(jnp.int32, (CHUNK, CHUNK), 0)
        col = lax.broadcasted_iota(jnp.int32, (CHUNK, CHUNK), 1)
        for g in range(GM_GROUPS):
            wspm[g] = jnp.where(row >= col, wsp_ref[g], 0.0).astype(BF16)

    @pl.when(pl.program_id(1) == 0)
    def _():
        tail_buf[...] = jnp.zeros(tail_buf.shape, F32)

    sh1, sc1, g1 = mod_ref[0:1, :], mod_ref[1:2, :], mod_ref[2:3, :]
    sh2, sc2 = mod_ref[3:4, :], mod_ref[4:5, :]
    gm_g, gm_b = vec_ref[0:1, :], vec_ref[1:2, :]
    cv_g, cv_b, conv_b = vec_ref[2:3, :], vec_ref[3:4, :], vec_ref[4:5, :]
    gd = d // GM_GROUPS
    off0 = HALO - CONV_K + 1
    conv_groups = CONV_ROWS // SUBLANES
    sched = _Interleaver()

    def qcols(q):
        return slice(q * qw, (q + 1) * qw)

    def perm_rows(r):
        return pl.ds((r % seg) * SUBLANES + r // seg, rb, stride=SUBLANES)

    def norm_item(r):
        def fn():
            xb = x_ref[r:r + rb, :]
            h = _rms(xb) * (1.0 + sc1) + sh1
            h_buf[r:r + rb, :] = h.astype(BF16)
            for s8 in range(d // LANES):
                hp32[s8, perm_rows(r), :] = h[:, s8 * LANES:(s8 + 1) * LANES]
        return fn
    for r in row_blocks:
        sched.add(("norm", r), "vpu", 90, [], norm_item(r))
    all_norm = [("norm", r) for r in row_blocks]

    def perm_pack_item(p0):
        def fn():
            for s8 in range(d // LANES):
                hp_buf[p0:p0 + rb, s8 * LANES:(s8 + 1) * LANES] = hp32[s8, p0:p0 + rb, :].astype(BF16)
        return fn
    for p0 in row_blocks:
        sched.add(("pack", p0), "vpu", 20, all_norm, perm_pack_item(p0))
    all_pack = [("pack", p0) for p0 in row_blocks]

    def in_dot_item(split, q, dst, lhs=h_buf):
        def fn():
            c0 = split * d + q * qw
            dst[:, qcols(q)] = jnp.dot(lhs[...], win_ref[:, c0:c0 + qw], preferred_element_type=F32)
        return fn

    def a_item(q):
        def fn():
            e_buf[HALO * SUBLANES:, qcols(q)] = z0[:, qcols(q)] * jax.nn.sigmoid(z1[:, qcols(q)])
        return fn

    def halo_item(q):
        def fn():
            shape3 = (HALO, SUBLANES, qw)
            cur = e_buf[seg * SUBLANES:, qcols(q)]
            prev = tail_buf[:, qcols(q)]
            first = lax.broadcasted_iota(jnp.int32, shape3, 1) == 0
            halo = jnp.where(first, pltpu.roll(prev.reshape(shape3), 1, axis=1),
                             pltpu.roll(cur.reshape(shape3), 1, axis=1))
            e_buf[0:HALO * SUBLANES, qcols(q)] = halo.reshape(HALO * SUBLANES, qw)
            tail_buf[:, qcols(q)] = cur
        return fn

    def conv_item(jb, q):
        def fn():
            for c0 in range(q * qw, (q + 1) * qw, CONV_LANES):
                cols = slice(c0, c0 + CONV_LANES)
                g0 = jb + off0
                win = e_buf[g0 * SUBLANES:(g0 + conv_groups + CONV_K - 1) * SUBLANES, cols]
                win = win.reshape(conv_groups + CONV_K - 1, SUBLANES, CONV_LANES)
                acc = jnp.broadcast_to(conv_b[:, cols], (conv_groups, SUBLANES, CONV_LANES))
                for k in range(CONV_K):
                    w_k = cwb_ref[k * SUBLANES:(k + 1) * SUBLANES, cols]
                    acc = acc + w_k[None] * win[k:k + conv_groups]
                zcp[c0 // LANES, jb * SUBLANES:(jb + conv_groups) * SUBLANES, :] = (
                    acc.reshape(CONV_ROWS, CONV_LANES))
        return fn

    def cln_item(r):
        def fn():
            y = jnp.concatenate([zcp[s8, perm_rows(r), :] for s8 in range(d // LANES)], axis=1)
            yb_buf[r:r + rb, :] = jax.nn.silu(_layer_norm(y, cv_g, cv_b)).astype(BF16)
        return fn

    def u_item(q):
        def fn():
            u_buf[:, qcols(q)] = jax.nn.gelu(z2[:, qcols(q)]).astype(BF16)
        return fn

    def v_item(r):
        def fn():
            v = jax.nn.gelu(z3[r:r + rb, :])
            v_buf[r:r + rb, :] = _layer_norm(v, gm_g, gm_b).astype(BF16)
        return fn

    def gate_item(q):
        def fn():
            ga_buf[:, qcols(q)] = jax.nn.sigmoid(z0[:, qcols(q)]).astype(BF16)
            gb_buf[:, qcols(q)] = jax.nn.sigmoid(z1[:, qcols(q)]).astype(BF16)
        return fn

    def spatial_item(ci):
        def fn():
            rows = slice(ci * CHUNK, (ci + 1) * CHUNK)
            for g in range(GM_GROUPS):
                cols = slice(g * gd, (g + 1) * gd)
                sv = jnp.dot(wspm[g], v_buf[rows, cols], preferred_element_type=F32) + bsp_ref[:, cols]
                ya_buf[rows, cols] = (u_buf[rows, cols].astype(F32) * sv).astype(BF16)
        return fn

    def out_dot_item(src, w_ref, dst, q, rows=slice(None)):
        def fn():
            dst[rows, qcols(q)] = jnp.dot(src[rows, :], w_ref[:, qcols(q)], preferred_element_type=F32)
        return fn

    def merge_item(q):
        def fn():
            m = (ga_buf[:, qcols(q)].astype(F32) * z2[:, qcols(q)]
                 + gb_buf[:, qcols(q)].astype(F32) * z3[:, qcols(q)])
            m_buf[:, qcols(q)] = m.astype(BF16)
        return fn

    def out_item(r):
        def fn():
            xn = x_ref[r:r + rb, :] + g1 * z0[r:r + rb, :]
            xo_ref[r:r + rb, :] = xn
            h2 = _rms(xn) * (1.0 + sc2) + sh2
            if moe:
                h2f[r:r + rb, :] = h2
                for s8 in range(d // LANES):
                    h2_ref[pl.ds(r * SUBLANES + s8, rb, stride=SUBLANES), :] = (
                        h2[:, s8 * LANES:(s8 + 1) * LANES])
            else:
                h2_ref[r:r + rb, :] = h2.astype(h2_ref.dtype)
        return fn

    conv_blocks = list(range(0, seg, conv_groups))
    chunks = list(range(ts // CHUNK))
    halves = [slice(0, ts // 2), slice(ts // 2, ts)]
    for q in range(nq):
        sched.add(("d_cva", q), "mxu", 512, all_pack, in_dot_item(2, q, z0, hp_buf))
        sched.add(("d_cvg", q), "mxu", 512, all_pack, in_dot_item(3, q, z1, hp_buf))
        sched.add(("a", q), "vpu", 300, [("d_cva", q), ("d_cvg", q)], a_item(q))
        sched.add(("halo", q), "vpu", 60, [("a", q)], halo_item(q))
    for q in range(nq):
        sched.add(("d_u", q), "mxu", 512, all_norm, in_dot_item(0, q, z2))
        sched.add(("u", q), "vpu", 350, [("d_u", q)], u_item(q))
    for q in range(nq):
        sched.add(("d_v", q), "mxu", 512, all_norm, in_dot_item(1, q, z3))
    for r in row_blocks:
        sched.add(("v", r), "vpu", 170, [("d_v", q) for q in range(nq)], v_item(r))
    for q in range(nq):
        sched.add(("d_ga", q), "mxu", 512, all_norm + [("a", q)], in_dot_item(4, q, z0))
        sched.add(("d_gb", q), "mxu", 512, all_norm + [("a", q)], in_dot_item(5, q, z1))
        sched.add(("g", q), "vpu", 550, [("d_ga", q), ("d_gb", q)], gate_item(q))
    for ci in chunks:
        deps = [("v", r) for r in row_blocks if r // CHUNK == ci] + [("u", q) for q in range(nq)]
        sched.add(("sp", ci), "mxu", 600, deps, spatial_item(ci))
    for q in range(nq):
        for jb in conv_blocks:
            deps = [("a", q)] + ([("halo", q)] if jb + off0 < HALO else [])
            sched.add(("conv", jb, q), "vpu", 260, deps, conv_item(jb, q))
    for r in row_blocks:
        j0 = r % seg
        deps = [("conv", jb, q) for jb in conv_blocks if j0 <= jb < j0 + rb for q in range(nq)]
        sched.add(("cln", r), "vpu", 180, deps, cln_item(r))
    all_sp = [("sp", ci) for ci in chunks]
    all_cln = [("cln", r) for r in row_blocks]
    all_v = [("v", r) for r in row_blocks]
    for q in range(nq):
        sched.add(("d_ya", q), "mxu", 512, all_sp, out_dot_item(ya_buf, wgm_ref, z2, q))
        sched.add(("d_yb", q), "mxu", 512, all_cln + all_v, out_dot_item(yb_buf, wcv_ref, z3, q))
        sched.add(("m", q), "vpu", 250, [("d_ya", q), ("d_yb", q), ("g", q)], merge_item(q))
    all_m = [("m", q) for q in range(nq)]
    for hi, rows in enumerate(halves):
        for q in range(nq):
            sched.add(("d_o", hi, q), "mxu", 256, all_m, out_dot_item(m_buf, wo_ref, z0, q, rows))
        for r in row_blocks:
            if rows.start <= r < rows.stop:
                sched.add(("out", r), "vpu", 100, [("d_o", hi, q) for q in range(nq)], out_item(r))
    sched.emit()

    if moe:
        nt_dims = (((1,), (1,)), ((), ()))
        hf = h2f[...]
        hi = hf.astype(BF16)
        lo = (hf - hi.astype(F32)).astype(BF16)
        part = (lax.dot_general(rw_ref[...], hi, nt_dims, preferred_element_type=F32)
                + lax.dot_general(rw_ref[...], lo, nt_dims, preferred_element_type=F32))
        lg = part[0:N_EXPERTS, :] + part[N_EXPERTS:2 * N_EXPERTS, :]
        ls = [lg[e:e + 1, :] for e in range(N_EXPERTS)]
        m1 = functools.reduce(jnp.maximum, ls)
        i1 = jnp.full(m1.shape, N_EXPERTS - 1, jnp.int32)
        for e in range(N_EXPERTS - 2, -1, -1):
            i1 = jnp.where(ls[e] == m1, e, i1)
        ls2 = [jnp.where(i1 == e, -jnp.inf, ls[e]) for e in range(N_EXPERTS)]
        m2 = functools.reduce(jnp.maximum, ls2)
        i2 = jnp.full(m1.shape, N_EXPERTS - 1, jnp.int32)
        for e in range(N_EXPERTS - 2, -1, -1):
            i2 = jnp.where(ls2[e] == m2, e, i2)
        e2 = jnp.exp(m2 - m1)
        den = 1.0 + e2
        ti_ref[0:1, :] = i1
        ti_ref[1:2, :] = i2
        tw_ref[0:1, :] = 1.0 / den
        tw_ref[1:2, :] = e2 / den


def rb_rows(seg):
    return min(32, seg)


def _mixer(l, moe, x, mod, win, wsp, bsp_full, vecs, cw, wgm, wcv, wo, rw_t):
    bsz, s, d = x.shape
    ts = min(MIX_ROWS, s)
    ns = s // ts
    t = bsz * s
    in_specs = [
        pl.BlockSpec((None, ts, d), lambda b, i: (b, i, 0)),
        pl.BlockSpec((None, None, 6, d), lambda b, i: (l, b, 0, 0)),
        _const_spec((None, d, 6 * d), lambda b, i: (l, 0, 0)),
        _const_spec((None, GM_GROUPS, CHUNK, CHUNK), lambda b, i: (l, 0, 0, 0)),
        _const_spec((None, CHUNK, d), lambda b, i: (l, 0, 0)),
        _const_spec((None, 8, d), lambda b, i: (l, 0, 0)),
        _const_spec((None, HALO * SUBLANES, d), lambda b, i: (l, 0, 0)),
        _const_spec((None, d, d), lambda b, i: (l, 0, 0)),
        _const_spec((None, d, d), lambda b, i: (l, 0, 0)),
        _const_spec((None, d, d), lambda b, i: (l, 0, 0)),
    ]
    args = [x, mod, win, wsp, bsp_full, vecs, cw, wgm, wcv, wo]
    tok_rows = d // LANES
    assert tok_rows == SUBLANES
    out_specs = [pl.BlockSpec((None, ts, d), lambda b, i: (b, i, 0))]
    out_shape = [jax.ShapeDtypeStruct((bsz, s, d), F32)]
    if moe:
        out_specs.append(pl.BlockSpec((ts * tok_rows, LANES), lambda b, i: (b * ns + i, 0)))
        out_shape.append(jax.ShapeDtypeStruct((t * tok_rows, LANES), F32))
    else:
        out_specs.append(pl.BlockSpec((None, ts, d), lambda b, i: (b, i, 0)))
        out_shape.append(jax.ShapeDtypeStruct((bsz, s, d), BF16))
    if moe:
        in_specs.append(_const_spec((None, 2 * N_EXPERTS, d), lambda b, i: (l // 2, 0, 0)))
        args.append(rw_t)
        out_specs += [pl.BlockSpec((2, ts), lambda b, i: (0, b * ns + i)),
                      pl.BlockSpec((2, ts), lambda b, i: (0, b * ns + i))]
        out_shape += [jax.ShapeDtypeStruct((2, t), jnp.int32), jax.ShapeDtypeStruct((2, t), F32)]
    scratch = [
        pltpu.VMEM((ts, d), BF16),
        pltpu.VMEM((d // LANES, ts, LANES), F32),
        pltpu.VMEM((ts, d), BF16),
        pltpu.VMEM((ts, d), BF16),
        pltpu.VMEM((ts, d), F32),
        pltpu.VMEM((ts, d), F32),
        pltpu.VMEM((ts, d), F32),
        pltpu.VMEM((ts, d), F32),
        pltpu.VMEM((d // LANES, ts, LANES), F32),
        pltpu.VMEM((ts, d), BF16),
        pltpu.VMEM((ts, d), BF16),
        pltpu.VMEM((ts + HALO * SUBLANES, d), F32),
        pltpu.VMEM((HALO * SUBLANES, d), F32),
        pltpu.VMEM((ts, d), BF16),
        pltpu.VMEM((ts, d), BF16),
        pltpu.VMEM((ts, d), BF16),
        pltpu.VMEM((ts, d), BF16),
        pltpu.VMEM((GM_GROUPS, CHUNK, CHUNK), BF16),
    ]
    if moe:
        scratch.append(pltpu.VMEM((ts, d), F32))
    return pl.pallas_call(
        functools.partial(_mixer_kernel, moe, ts, d),
        grid=(bsz, ns),
        in_specs=in_specs,
        out_specs=out_specs,
        out_shape=out_shape,
        scratch_shapes=scratch,
        compiler_params=pltpu.CompilerParams(
            dimension_semantics=("arbitrary", "arbitrary"), vmem_limit_bytes=VMEM_LIMIT_BYTES),
        name="mixer_moe" if moe else "mixer_dense",
    )(*args)


def _swiglu_tile(x_ref, w1_ref, w3_ref, w2_ref, g_buf, fc):
    f = w1_ref.shape[-1]
    x = x_ref[...]
    for c0 in range(0, f, fc):
        cols = slice(c0, c0 + fc)
        h1 = jnp.dot(x, w1_ref[:, cols], preferred_element_type=F32)
        h3 = jnp.dot(x, w3_ref[:, cols], preferred_element_type=F32)
        g_buf[:, cols] = (jax.nn.silu(h1) * h3).astype(BF16)
    return jnp.dot(g_buf[...], w2_ref[...], preferred_element_type=F32)


def _ffn_dense_kernel(fc, x_ref, h_ref, mod_ref, w1_ref, w3_ref, w2_ref, o_ref, g_buf):
    g2 = mod_ref[5:6, :]
    o_ref[...] = x_ref[...] + g2 * _swiglu_tile(h_ref, w1_ref, w3_ref, w2_ref, g_buf, fc)


def _ffn_dense(l, x, h2, mod, w1, w3, w2):
    t, d = x.shape
    f = w1.shape[-1]
    s = t // mod.shape[1]
    tm = min(FFN_ROWS, s)
    j = l // 2
    return pl.pallas_call(
        functools.partial(_ffn_dense_kernel, FFN_COLS_DENSE),
        grid=(t // tm,),
        in_specs=[
            pl.BlockSpec((tm, d), lambda i: (i, 0)),
            pl.BlockSpec((tm, d), lambda i: (i, 0)),
            pl.BlockSpec((None, None, 6, d), lambda i: (l, (i * tm) // s, 0, 0)),
            _const_spec((None, d, f), lambda i: (j, 0, 0)),
            _const_spec((None, d, f), lambda i: (j, 0, 0)),
            _const_spec((None, f, d), lambda i: (j, 0, 0)),
        ],
        out_specs=pl.BlockSpec((tm, d), lambda i: (i, 0)),
        out_shape=jax.ShapeDtypeStruct((t, d), F32),
        scratch_shapes=[pltpu.VMEM((tm, f), BF16)],
        compiler_params=pltpu.CompilerParams(
            dimension_semantics=("arbitrary",), vmem_limit_bytes=VMEM_LIMIT_BYTES),
        name="ffn_dense",
    )(x, h2, mod, w1, w3, w2)


def _token_copy(src, src_row, dst, dst_row, sem):
    return pltpu.make_async_copy(src.at[pl.ds(pl.multiple_of(src_row, SUBLANES), SUBLANES), :],
                                 dst.at[pl.ds(pl.multiple_of(dst_row, SUBLANES), SUBLANES), :], sem)


def _ffn_moe_kernel(tm, fc, te_ref, nu_ref, nv_ref, idx_hbm, h_hbm, w1_ref, w3_ref, w2_ref, y_hbm,
                    idx_smem, xbuf, xbf, g_buf, ybuf, idx_sem, in_sem, out_sem):
    i = pl.program_id(0)
    n_used = nu_ref[0]
    slot = lax.rem(i, 2)
    d = xbf.shape[1]

    def idx_base(sl):
        return pl.multiple_of(sl * (2 * tm), 2 * tm)

    def idx_copy(tile, sl):
        return pltpu.make_async_copy(idx_hbm.at[tile], idx_smem.at[pl.ds(idx_base(sl), 2 * tm)],
                                     idx_sem.at[sl])

    def start_gathers(sl):
        base = idx_base(sl)
        def body(rg, carry):
            for u in range(DMA_UNROLL):
                r = rg * DMA_UNROLL + u
                _token_copy(h_hbm, idx_smem[base + r], xbuf.at[sl], r * SUBLANES,
                            in_sem.at[sl]).start(priority=u % 2)
            return carry
        lax.fori_loop(0, tm // DMA_UNROLL, body, 0)

    def wait_gathers(sl):
        pltpu.make_async_copy(h_hbm.at[pl.ds(0, tm * SUBLANES), :], xbuf.at[sl], in_sem.at[sl]).wait()

    def scatter_copy(sl, r):
        return _token_copy(ybuf.at[sl], r * SUBLANES, y_hbm, idx_smem[idx_base(sl) + tm + r],
                           out_sem.at[sl])

    def start_scatters(sl, n_valid):
        @pl.when(n_valid == tm)
        def _():
            def body(rg, carry):
                for u in range(DMA_UNROLL):
                    scatter_copy(sl, rg * DMA_UNROLL + u).start(priority=u % 2)
                return carry
            lax.fori_loop(0, tm // DMA_UNROLL, body, 0)
        @pl.when(n_valid < tm)
        def _():
            def body(r, carry):
                scatter_copy(sl, r).start()
                return carry
            lax.fori_loop(0, n_valid, body, 0)

    def wait_scatters(sl, n_valid):
        @pl.when(n_valid == tm)
        def _():
            pltpu.make_async_copy(ybuf.at[sl], y_hbm.at[pl.ds(0, tm * SUBLANES), :], out_sem.at[sl]).wait()
        @pl.when(n_valid < tm)
        def _():
            def body(r, carry):
                _token_copy(ybuf.at[sl], r * SUBLANES, y_hbm, 0, out_sem.at[sl]).wait()
                return carry
            lax.fori_loop(0, n_valid, body, 0)

    @pl.when(i == 0)
    def _():
        idx_copy(0, 0).start()
        idx_copy(0, 0).wait()
        start_gathers(0)
        @pl.when(n_used > 1)
        def _():
            idx_copy(1, 1).start()

    @pl.when(i < n_used)
    def _():
        wait_gathers(slot)
        @pl.when(i + 1 < n_used)
        def _():
            idx_copy(i + 1, 1 - slot).wait()
            start_gathers(1 - slot)
        for s8 in range(d // LANES):
            xbf[:, s8 * LANES:(s8 + 1) * LANES] = (
                xbuf[slot, pl.ds(s8, tm, stride=SUBLANES), :].astype(BF16))
        y = _swiglu_tile(xbf, w1_ref, w3_ref, w2_ref, g_buf, fc)
        @pl.when(i >= 2)
        def _():
            wait_scatters(slot, nv_ref[jnp.maximum(i - 2, 0)])
        for s8 in range(d // LANES):
            ybuf[slot, pl.ds(s8, tm, stride=SUBLANES), :] = y[:, s8 * LANES:(s8 + 1) * LANES]
        start_scatters(slot, nv_ref[i])
        @pl.when(i + 2 < n_used)
        def _():
            idx_copy(i + 2, slot).start()
        @pl.when(i == n_used - 1)
        def _():
            @pl.when(i >= 1)
            def _():
                wait_scatters(1 - slot, nv_ref[jnp.maximum(i - 1, 0)])
            wait_scatters(slot, nv_ref[i])


def _ffn_moe(jl, h2t, idx, tile_expert, n_used, n_valid, w1, w3, w2, tm, nt, n_out_rows):
    d, f = w1.shape[-2:]
    grid_spec = pltpu.PrefetchScalarGridSpec(
        num_scalar_prefetch=3,
        grid=(nt,),
        in_specs=[
            pl.BlockSpec(memory_space=pl.ANY),
            pl.BlockSpec(memory_space=pl.ANY),
            _const_spec((None, None, d, f), lambda i, te, nu, nv: (jl, te[i], 0, 0)),
            _const_spec((None, None, d, f), lambda i, te, nu, nv: (jl, te[i], 0, 0)),
            _const_spec((None, None, f, d), lambda i, te, nu, nv: (jl, te[i], 0, 0)),
        ],
        out_specs=pl.BlockSpec(memory_space=pl.ANY),
        scratch_shapes=[
            pltpu.SMEM((4 * tm,), jnp.int32),
            pltpu.VMEM((2, tm * SUBLANES, LANES), F32),
            pltpu.VMEM((tm, d), BF16),
            pltpu.VMEM((tm, f), BF16),
            pltpu.VMEM((2, tm * SUBLANES, LANES), F32),
            pltpu.SemaphoreType.DMA((2,)), pltpu.SemaphoreType.DMA((2,)),
            pltpu.SemaphoreType.DMA((2,)),
        ],
    )
    return pl.pallas_call(
        functools.partial(_ffn_moe_kernel, tm, FFN_COLS_MOE),
        grid_spec=grid_spec,
        out_shape=jax.ShapeDtypeStruct((n_out_rows, LANES), F32),
        compiler_params=pltpu.CompilerParams(
            dimension_semantics=("arbitrary",), vmem_limit_bytes=VMEM_LIMIT_BYTES,
            disable_bounds_checks=True, has_side_effects=True),
        name="ffn_moe",
    )(tile_expert, n_used, n_valid, idx, h2t, w1, w3, w2)


def _combine_kernel(tc, d, final, x_ref, y0_ref, y1_ref, tw_ref, mod_ref, fg_ref, o_ref):
    g2 = mod_ref[5:6, :]
    tw = tw_ref[...]
    parts = []
    for s8 in range(d // LANES):
        y0 = y0_ref[pl.ds(s8, tc, stride=SUBLANES), :]
        y1 = y1_ref[pl.ds(s8, tc, stride=SUBLANES), :]
        parts.append(tw[:, 0:1] * y0 + tw[:, 1:2] * y1)
    xn = x_ref[...] + g2 * jnp.concatenate(parts, axis=1)
    if final:
        xn = _rms(xn) * fg_ref[...]
    o_ref[...] = xn


def _combine(l, final, x, y2, tw_t, mod, final_g):
    t, d = x.shape
    s = t // mod.shape[1]
    tc = min(COMBINE_ROWS, s)
    nt = t // tc
    return pl.pallas_call(
        functools.partial(_combine_kernel, tc, d, final),
        grid=(nt,),
        in_specs=[
            pl.BlockSpec((tc, d), lambda i: (i, 0)),
            pl.BlockSpec((tc * SUBLANES, LANES), lambda i: (i, 0)),
            pl.BlockSpec((tc * SUBLANES, LANES), lambda i: (nt + i, 0)),
            pl.BlockSpec((tc, 2), lambda i: (i, 0)),
            pl.BlockSpec((None, None, 6, d), lambda i: (l, (i * tc) // s, 0, 0)),
            pl.BlockSpec((1, d), lambda i: (0, 0)),
        ],
        out_specs=pl.BlockSpec((tc, d), lambda i: (i, 0)),
        out_shape=jax.ShapeDtypeStruct((t, d), F32),
        compiler_params=pltpu.CompilerParams(
            dimension_semantics=("arbitrary",), vmem_limit_bytes=VMEM_LIMIT_BYTES),
        name="moe_combine",
    )(x, y2, y2, tw_t, mod, final_g.reshape(1, d))


def _dispatch_plan(ti, tm, nt):
    t = ti.shape[1]
    e_flat = ti.reshape(-1)
    experts = jnp.arange(N_EXPERTS, dtype=jnp.int32)
    counts = jnp.sum((e_flat[:, None] == experts[None, :]).astype(jnp.int32), axis=0)
    order = jnp.argsort(e_flat, stable=True).astype(jnp.int32)
    group_start = jnp.cumsum(counts) - counts
    tiles_e = (counts + tm - 1) // tm
    tile_end = jnp.cumsum(tiles_e)
    tile_start = tile_end - tiles_e
    n_used = tile_end[-1]
    tile_id = jnp.minimum(jnp.arange(nt, dtype=jnp.int32), n_used - 1)
    tile_expert = jnp.minimum(
        jnp.searchsorted(tile_end, tile_id, side="right").astype(jnp.int32), N_EXPERTS - 1)
    first_row = (jnp.arange(nt, dtype=jnp.int32) - tile_start[tile_expert]) * tm
    n_valid = jnp.clip(counts[tile_expert] - first_row, 0, tm)
    n_valid = jnp.where(jnp.arange(nt) < n_used, n_valid, 0).astype(jnp.int32)
    p = first_row[:, None] + jnp.arange(tm, dtype=jnp.int32)[None, :]
    valid = jnp.arange(tm, dtype=jnp.int32)[None, :] < n_valid[:, None]
    a = order[jnp.clip(group_start[tile_expert][:, None] + p, 0, 2 * t - 1)]
    src_row = jnp.where(valid, a % t, 0) * SUBLANES
    dst_row = jnp.where(valid, a, 0) * SUBLANES
    idx = jnp.concatenate([src_row, dst_row], axis=1).astype(jnp.int32)
    return idx, tile_expert, n_used.reshape(1).astype(jnp.int32), n_valid


def kernel(x, c, ada_w, ada_b, w_in, gm_ln_g, gm_ln_b, w_sp, b_sp, w_gm_out, conv_w, conv_b,
           cv_ln_g, cv_ln_b, w_cv_out, w_o, ffn_w1, ffn_w3, ffn_w2, router_w, exp_w1, exp_w3,
           exp_w2, final_g):
    bsz, s, d = x.shape
    depth = ada_w.shape[0]
    t = bsz * s
    assert s % CHUNK == 0 and d % GM_GROUPS == 0 and depth % 2 == 0

    mod = _ada_modulation(c, ada_w, ada_b).reshape(depth, bsz, 6, d)

    win = w_in.astype(BF16)
    wgm, wcv, wo = w_gm_out.astype(BF16), w_cv_out.astype(BF16), w_o.astype(BF16)
    bsp_full = jnp.repeat(jnp.swapaxes(b_sp, 1, 2), d // GM_GROUPS, axis=2)
    zeros = jnp.zeros_like(conv_b)
    vecs = jnp.stack([gm_ln_g, gm_ln_b, cv_ln_g, cv_ln_b, conv_b, zeros, zeros, zeros], axis=1)
    cw = jnp.pad(conv_w, ((0, 0), (0, HALO - CONV_K), (0, 0)))
    cw = jnp.repeat(cw, SUBLANES, axis=1)
    rw_t = jnp.swapaxes(router_w, 1, 2)
    rw_hi = rw_t.astype(BF16)
    rw_t = jnp.concatenate([rw_hi, (rw_t - rw_hi.astype(F32)).astype(BF16)], axis=1)
    fw1, fw3, fw2 = ffn_w1.astype(BF16), ffn_w3.astype(BF16), ffn_w2.astype(BF16)
    ew1, ew3, ew2 = exp_w1.astype(BF16), exp_w3.astype(BF16), exp_w2.astype(BF16)

    tm = min(FFN_ROWS, t)
    nt = (2 * t) // tm + N_EXPERTS

    for l in range(depth):
        moe = l % 2 == 1
        outs = _mixer(l, moe, x, mod, win, w_sp, bsp_full, vecs, cw, wgm, wcv, wo, rw_t)
        if not moe:
            xn, h2 = outs
            x2 = _ffn_dense(l, xn.reshape(t, d), h2.reshape(t, d), mod, fw1, fw3, fw2)
        else:
            xn, h2t, ti, tw = outs
            idx, tile_expert, n_used, n_valid = _dispatch_plan(ti, tm, nt)
            y2 = _ffn_moe(l // 2, h2t, idx, tile_expert, n_used, n_valid, ew1, ew3, ew2, tm, nt,
                          2 * t * SUBLANES)
            x2 = _combine(l, l == depth - 1, xn.reshape(t, d), y2, tw.T, mod, final_g)
        x = x2.reshape(bsz, s, d)
    return x
```

```python
import functools

import jax
import jax.numpy as jnp
from jax import lax
from jax.experimental import pallas as pl
from jax.experimental.pallas import tpu as pltpu

F32 = jnp.float32
BF16 = jnp.bfloat16

EPS = 1e-6
CHUNK = 128
GM_GROUPS = 8
CONV_K = 31
N_EXPERTS = 8
HALO = 32
CONV_ROWS = 64
CONV_LANES = 128
SUBLANES = 8
LANES = 128
VMEM_LIMIT_BYTES = 56 * 1024 * 1024

MIX_ROWS = 512
FFN_ROWS = 512
FFN_COLS_DENSE = 256
FFN_COLS_MOE = 512
COMBINE_ROWS = 256
DMA_UNROLL = 8
IDX_SLOTS = 3


def _const_spec(shape, index_map):
    return pl.BlockSpec(shape, index_map, pipeline_mode=pl.Buffered(1))


def _rms(x):
    return x * lax.rsqrt(jnp.mean(x * x, axis=-1, keepdims=True) + EPS)


def _layer_norm(x, g, b):
    mu = jnp.mean(x, axis=-1, keepdims=True)
    d = x - mu
    var = jnp.mean(d * d, axis=-1, keepdims=True)
    return d * lax.rsqrt(var + EPS) * g + b


def _ada_kernel(c_ref, w_ref, b_ref, o_ref):
    sc = jax.nn.silu(c_ref[...])
    o_ref[...] = jnp.dot(sc, w_ref[...], precision=lax.Precision.HIGHEST,
                         preferred_element_type=F32) + b_ref[...]


def _ada_modulation(c, ada_w, ada_b):
    depth, d, d6 = ada_w.shape
    bsz = c.shape[0]
    nj = d6 // d
    return pl.pallas_call(
        _ada_kernel,
        grid=(depth, nj),
        in_specs=[
            pl.BlockSpec((bsz, d), lambda l, j: (0, 0)),
            pl.BlockSpec((None, d, d), lambda l, j: (l, 0, j)),
            pl.BlockSpec((None, 1, d), lambda l, j: (l, 0, j)),
        ],
        out_specs=pl.BlockSpec((None, bsz, d), lambda l, j: (l, 0, j)),
        out_shape=jax.ShapeDtypeStruct((depth, bsz, d6), F32),
        compiler_params=pltpu.CompilerParams(
            dimension_semantics=("arbitrary", "arbitrary"), vmem_limit_bytes=VMEM_LIMIT_BYTES),
        name="ada_modulation",
    )(c, ada_w, ada_b.reshape(depth, 1, d6))


class _Interleaver:
    def __init__(self):
        self.items = []

    def add(self, name, unit, cost, deps, fn):
        self.items.append((name, unit, cost, tuple(deps), fn))

    def emit(self):
        done, unit_time, pending = {}, {"mxu": 0, "vpu": 0}, list(self.items)
        while pending:
            best = None
            for it in pending:
                if all(dep in done for dep in it[3]):
                    start = max([unit_time[it[1]]] + [done[dep] for dep in it[3]])
                    if best is None or start < best[0]:
                        best = (start, it)
            start, it = best
            it[4]()
            done[it[0]] = unit_time[it[1]] = start + it[2]
            pending.remove(it)


def _mixer_kernel(moe, ts, d, x_ref, mod_ref, win_ref, wsp_ref, bsp_ref, vec_ref, cwb_ref,
                  wgm_ref, wcv_ref, wo_ref, *rest):
    if moe:
        rw_ref, xo_ref, h2_ref, ti_ref, tw_ref = rest[:5]
        scratch = rest[5:]
    else:
        xo_ref, h2_ref = rest[:2]
        scratch = rest[2:]
    (h_buf, hp32, hp_buf, m_buf, z0, z1, z2, z3, zcp, u_buf, v_buf, e_buf, tail_buf, ga_buf, gb_buf,
     ya_buf, yb_buf, wspm) = scratch[:18]
    h2f = scratch[18] if moe else None
    seg = ts // SUBLANES
    assert seg % rb_rows(seg) == 0 and seg >= HALO
    rb = rb_rows(seg)
    qw = 256
    nq = d // qw
    row_blocks = list(range(0, ts, rb))
    first_tile = jnp.logical_and(pl.program_id(0) == 0, pl.program_id(1) == 0)

    @pl.when(first_tile)
    def _():
        row = lax.broadcasted_iota(jnp.int32, (CHUNK, CHUNK), 0)
        col = lax.broadcasted_iota(jnp.int32, (CHUNK, CHUNK), 1)
        for g in range(GM_GROUPS):
            wspm[g] = jnp.where(row >= col, wsp_ref[g], 0.0).astype(BF16)

    @pl.when(pl.program_id(1) == 0)
    def _():
        tail_buf[...] = jnp.zeros(tail_buf.shape, F32)

    sh1, sc1, g1 = mod_ref[0:1, :], mod_ref[1:2, :], mod_ref[2:3, :]
    sh2, sc2 = mod_ref[3:4, :], mod_ref[4:5, :]
    gm_g, gm_b = vec_ref[0:1, :], vec_ref[1:2, :]
    cv_g, cv_b, conv_b = vec_ref[2:3, :], vec_ref[3:4, :], vec_ref[4:5, :]
    gd = d // GM_GROUPS
    off0 = HALO - CONV_K + 1
    conv_groups = CONV_ROWS // SUBLANES
    sched = _Interleaver()

    def qcols(q):
        return slice(q * qw, (q + 1) * qw)

    def perm_rows(r):
        return pl.ds((r % seg) * SUBLANES + r // seg, rb, stride=SUBLANES)

    def norm_item(r):
        def fn():
            xb = x_ref[r:r + rb, :]
            h = _rms(xb) * (1.0 + sc1) + sh1
            h_buf[r:r + rb, :] = h.astype(BF16)
            for s8 in range(d // LANES):
                hp32[s8, perm_rows(r), :] = h[:, s8 * LANES:(s8 + 1) * LANES]
        return fn
    for r in row_blocks:
        sched.add(("norm", r), "vpu", 90, [], norm_item(r))
    all_norm = [("norm", r) for r in row_blocks]

    def perm_pack_item(p0):
        def fn():
            for s8 in range(d // LANES):
                hp_buf[p0:p0 + rb, s8 * LANES:(s8 + 1) * LANES] = hp32[s8, p0:p0 + rb, :].astype(BF16)
        return fn
    for p0 in row_blocks:
        sched.add(("pack", p0), "vpu", 20, all_norm, perm_pack_item(p0))
    all_pack = [("pack", p0) for p0 in row_blocks]

    def in_dot_item(split, q, dst, lhs=h_buf):
        def fn():
            c0 = split * d + q * qw
            dst[:, qcols(q)] = jnp.dot(lhs[...], win_ref[:, c0:c0 + qw], preferred_element_type=F32)
        return fn

    def a_item(q):
        def fn():
            e_buf[HALO * SUBLANES:, qcols(q)] = z0[:, qcols(q)] * jax.nn.sigmoid(z1[:, qcols(q)])
        return fn

    def halo_item(q):
        def fn():
            shape3 = (HALO, SUBLANES, qw)
            cur = e_buf[seg * SUBLANES:, qcols(q)]
            prev = tail_buf[:, qcols(q)]
            first = lax.broadcasted_iota(jnp.int32, shape3, 1) == 0
            halo = jnp.where(first, pltpu.roll(prev.reshape(shape3), 1, axis=1),
                             pltpu.roll(cur.reshape(shape3), 1, axis=1))
            e_buf[0:HALO * SUBLANES, qcols(q)] = halo.reshape(HALO * SUBLANES, qw)
            tail_buf[:, qcols(q)] = cur
        return fn

    def conv_item(jb, q):
        def fn():
            for c0 in range(q * qw, (q + 1) * qw, CONV_LANES):
                cols = slice(c0, c0 + CONV_LANES)
                g0 = jb + off0
                win = e_buf[g0 * SUBLANES:(g0 + conv_groups + CONV_K - 1) * SUBLANES, cols]
                win = win.reshape(conv_groups + CONV_K - 1, SUBLANES, CONV_LANES)
                acc = jnp.broadcast_to(conv_b[:, cols], (conv_groups, SUBLANES, CONV_LANES))
                for k in range(CONV_K):
                    w_k = cwb_ref[k * SUBLANES:(k + 1) * SUBLANES, cols]
                    acc = acc + w_k[None] * win[k:k + conv_groups]
                zcp[c0 // LANES, jb * SUBLANES:(jb + conv_groups) * SUBLANES, :] = (
                    acc.reshape(CONV_ROWS, CONV_LANES))
        return fn

    def cln_item(r):
        def fn():
            y = jnp.concatenate([zcp[s8, perm_rows(r), :] for s8 in range(d // LANES)], axis=1)
            yb_buf[r:r + rb, :] = jax.nn.silu(_layer_norm(y, cv_g, cv_b)).astype(BF16)
        return fn

    def u_item(q):
        def fn():
            u_buf[:, qcols(q)] = jax.nn.gelu(z2[:, qcols(q)]).astype(BF16)
        return fn

    def v_item(r):
        def fn():
            v = jax.nn.gelu(z3[r:r + rb, :])
            v_buf[r:r + rb, :] = _layer_norm(v, gm_g, gm_b).astype(BF16)
        return fn

    def gate_item(q):
        def fn():
            ga_buf[:, qcols(q)] = jax.nn.sigmoid(z0[:, qcols(q)]).astype(BF16)
            gb_buf[:, qcols(q)] = jax.nn.sigmoid(z1[:, qcols(q)]).astype(BF16)
        return fn

    def spatial_item(ci):
        def fn():
            rows = slice(ci * CHUNK, (ci + 1) * CHUNK)
            for g in range(GM_GROUPS):
                cols = slice(g * gd, (g + 1) * gd)
                sv = jnp.dot(wspm[g], v_buf[rows, cols], preferred_element_type=F32) + bsp_ref[:, cols]
                ya_buf[rows, cols] = (u_buf[rows, cols].astype(F32) * sv).astype(BF16)
        return fn

    def out_dot_item(src, w_ref, dst, q, rows=slice(None)):
        def fn():
            dst[rows, qcols(q)] = jnp.dot(src[rows, :], w_ref[:, qcols(q)], preferred_element_type=F32)
        return fn

    def merge_item(q):
        def fn():
            m = (ga_buf[:, qcols(q)].astype(F32) * z2[:, qcols(q)]
                 + gb_buf[:, qcols(q)].astype(F32) * z3[:, qcols(q)])
            m_buf[:, qcols(q)] = m.astype(BF16)
        return fn

    def out_item(r):
        def fn():
            xn = x_ref[r:r + rb, :] + g1 * z0[r:r + rb, :]
            xo_ref[r:r + rb, :] = xn
            h2 = _rms(xn) * (1.0 + sc2) + sh2
            if moe:
                h2f[r:r + rb, :] = h2
                for s8 in range(d // LANES):
                    h2_ref[pl.ds(r * SUBLANES + s8, rb, stride=SUBLANES), :] = (
                        h2[:, s8 * LANES:(s8 + 1) * LANES])
            else:
                h2_ref[r:r + rb, :] = h2.astype(h2_ref.dtype)
        return fn

    conv_blocks = list(range(0, seg, conv_groups))
    chunks = list(range(ts // CHUNK))
    halves = [slice(0, ts // 2), slice(ts // 2, ts)]
    for q in range(nq):
        sched.add(("d_cva", q), "mxu", 512, all_pack, in_dot_item(2, q, z0, hp_buf))
        sched.add(("d_cvg", q), "mxu", 512, all_pack, in_dot_item(3, q, z1, hp_buf))
        sched.add(("a", q), "vpu", 300, [("d_cva", q), ("d_cvg", q)], a_item(q))
        sched.add(("halo", q), "vpu", 60, [("a", q)], halo_item(q))
    for q in range(nq):
        sched.add(("d_u", q), "mxu", 512, all_norm, in_dot_item(0, q, z2))
        sched.add(("u", q), "vpu", 350, [("d_u", q)], u_item(q))
    for q in range(nq):
        sched.add(("d_v", q), "mxu", 512, all_norm, in_dot_item(1, q, z3))
    for r in row_blocks:
        sched.add(("v", r), "vpu", 170, [("d_v", q) for q in range(nq)], v_item(r))
    for q in range(nq):
        sched.add(("d_ga", q), "mxu", 512, all_norm + [("a", q)], in_dot_item(4, q, z0))
        sched.add(("d_gb", q), "mxu", 512, all_norm + [("a", q)], in_dot_item(5, q, z1))
        sched.add(("g", q), "vpu", 550, [("d_ga", q), ("d_gb", q)], gate_item(q))
    for ci in chunks:
        deps = [("v", r) for r in row_blocks if r // CHUNK == ci] + [("u", q) for q in range(nq)]
        sched.add(("sp", ci), "mxu", 600, deps, spatial_item(ci))
    for q in range(nq):
        for jb in conv_blocks:
            deps = [("a", q)] + ([("halo", q)] if jb + off0 < HALO else [])
            sched.add(("conv", jb, q), "vpu", 260, deps, conv_item(jb, q))
    for r in row_blocks:
        j0 = r % seg
        deps = [("conv", jb, q) for jb in conv_blocks if j0 <= jb < j0 + rb for q in range(nq)]
        sched.add(("cln", r), "vpu", 180, deps, cln_item(r))
    all_sp = [("sp", ci) for ci in chunks]
    all_cln = [("cln", r) for r in row_blocks]
    all_v = [("v", r) for r in row_blocks]
    for q in range(nq):
        sched.add(("d_ya", q), "mxu", 512, all_sp, out_dot_item(ya_buf, wgm_ref, z2, q))
        sched.add(("d_yb", q), "mxu", 512, all_cln + all_v, out_dot_item(yb_buf, wcv_ref, z3, q))
        sched.add(("m", q), "vpu", 250, [("d_ya", q), ("d_yb", q), ("g", q)], merge_item(q))
    all_m = [("m", q) for q in range(nq)]
    for hi, rows in enumerate(halves):
        for q in range(nq):
            sched.add(("d_o", hi, q), "mxu", 256, all_m, out_dot_item(m_buf, wo_ref, z0, q, rows))
        for r in row_blocks:
            if rows.start <= r < rows.stop:
                sched.add(("out", r), "vpu", 100, [("d_o", hi, q) for q in range(nq)], out_item(r))
    sched.emit()

    if moe:
        nt_dims = (((1,), (1,)), ((), ()))
        hf = h2f[...]
        hi = hf.astype(BF16)
        lo = (hf - hi.astype(F32)).astype(BF16)
        part = (lax.dot_general(rw_ref[...], hi, nt_dims, preferred_element_type=F32)
                + lax.dot_general(rw_ref[...], lo, nt_dims, preferred_element_type=F32))
        lg = part[0:N_EXPERTS, :] + part[N_EXPERTS:2 * N_EXPERTS, :]
        ls = [lg[e:e + 1, :] for e in range(N_EXPERTS)]
        m1 = functools.reduce(jnp.maximum, ls)
        i1 = jnp.full(m1.shape, N_EXPERTS - 1, jnp.int32)
        for e in range(N_EXPERTS - 2, -1, -1):
            i1 = jnp.where(ls[e] == m1, e, i1)
        ls2 = [jnp.where(i1 == e, -jnp.inf, ls[e]) for e in range(N_EXPERTS)]
        m2 = functools.reduce(jnp.maximum, ls2)
        i2 = jnp.full(m1.shape, N_EXPERTS - 1, jnp.int32)
        for e in range(N_EXPERTS - 2, -1, -1):
            i2 = jnp.where(ls2[e] == m2, e, i2)
        e2 = jnp.exp(m2 - m1)
        den = 1.0 + e2
        ti_ref[0:1, :] = i1
        ti_ref[1:2, :] = i2
        tw_ref[0:1, :] = 1.0 / den
        tw_ref[1:2, :] = e2 / den


def rb_rows(seg):
    return min(32, seg)


def _mixer(l, moe, x, mod, win, wsp, bsp_full, vecs, cw, wgm, wcv, wo, rw_t):
    bsz, s, d = x.shape
    ts = min(MIX_ROWS, s)
    ns = s // ts
    t = bsz * s
    in_specs = [
        pl.BlockSpec((None, ts, d), lambda b, i: (b, i, 0)),
        pl.BlockSpec((None, None, 6, d), lambda b, i: (l, b, 0, 0)),
        _const_spec((None, d, 6 * d), lambda b, i: (l, 0, 0)),
        _const_spec((None, GM_GROUPS, CHUNK, CHUNK), lambda b, i: (l, 0, 0, 0)),
        _const_spec((None, CHUNK, d), lambda b, i: (l, 0, 0)),
        _const_spec((None, 8, d), lambda b, i: (l, 0, 0)),
        _const_spec((None, HALO * SUBLANES, d), lambda b, i: (l, 0, 0)),
        _const_spec((None, d, d), lambda b, i: (l, 0, 0)),
        _const_spec((None, d, d), lambda b, i: (l, 0, 0)),
        _const_spec((None, d, d), lambda b, i: (l, 0, 0)),
    ]
    args = [x, mod, win, wsp, bsp_full, vecs, cw, wgm, wcv, wo]
    tok_rows = d // LANES
    assert tok_rows == SUBLANES
    out_specs = [pl.BlockSpec((None, ts, d), lambda b, i: (b, i, 0))]
    out_shape = [jax.ShapeDtypeStruct((bsz, s, d), F32)]
    if moe:
        out_specs.append(pl.BlockSpec((ts * tok_rows, LANES), lambda b, i: (b * ns + i, 0)))
        out_shape.append(jax.ShapeDtypeStruct((t * tok_rows, LANES), F32))
    else:
        out_specs.append(pl.BlockSpec((None, ts, d), lambda b, i: (b, i, 0)))
        out_shape.append(jax.ShapeDtypeStruct((bsz, s, d), BF16))
    if moe:
        in_specs.append(_const_spec((None, 2 * N_EXPERTS, d), lambda b, i: (l // 2, 0, 0)))
        args.append(rw_t)
        out_specs += [pl.BlockSpec((2, ts), lambda b, i: (0, b * ns + i)),
                      pl.BlockSpec((2, ts), lambda b, i: (0, b * ns + i))]
        out_shape += [jax.ShapeDtypeStruct((2, t), jnp.int32), jax.ShapeDtypeStruct((2, t), F32)]
    scratch = [
        pltpu.VMEM((ts, d), BF16),
        pltpu.VMEM((d // LANES, ts, LANES), F32),
        pltpu.VMEM((ts, d), BF16),
        pltpu.VMEM((ts, d), BF16),
        pltpu.VMEM((ts, d), F32),
        pltpu.VMEM((ts, d), F32),
        pltpu.VMEM((ts, d), F32),
        pltpu.VMEM((ts, d), F32),
        pltpu.VMEM((d // LANES, ts, LANES), F32),
        pltpu.VMEM((ts, d), BF16),
        pltpu.VMEM((ts, d), BF16),
        pltpu.VMEM((ts + HALO * SUBLANES, d), F32),
        pltpu.VMEM((HALO * SUBLANES, d), F32),
        pltpu.VMEM((ts, d), BF16),
        pltpu.VMEM((ts, d), BF16),
        pltpu.VMEM((ts, d), BF16),
        pltpu.VMEM((ts, d), BF16),
        pltpu.VMEM((GM_GROUPS, CHUNK, CHUNK), BF16),
    ]
    if moe:
        scratch.append(pltpu.VMEM((ts, d), F32))
    return pl.pallas_call(
        functools.partial(_mixer_kernel, moe, ts, d),
        grid=(bsz, ns),
        in_specs=in_specs,
        out_specs=out_specs,
        out_shape=out_shape,
        scratch_shapes=scratch,
        compiler_params=pltpu.CompilerParams(
            dimension_semantics=("arbitrary", "arbitrary"), vmem_limit_bytes=VMEM_LIMIT_BYTES),
        name="mixer_moe" if moe else "mixer_dense",
    )(*args)


def _swiglu_tile(x_ref, w1_ref, w3_ref, w2_ref, g_buf, fc):
    f = w1_ref.shape[-1]
    x = x_ref[...]
    for c0 in range(0, f, fc):
        cols = slice(c0, c0 + fc)
        h1 = jnp.dot(x, w1_ref[:, cols], preferred_element_type=F32)
        h3 = jnp.dot(x, w3_ref[:, cols], preferred_element_type=F32)
        g_buf[:, cols] = (jax.nn.silu(h1) * h3).astype(BF16)
    return jnp.dot(g_buf[...], w2_ref[...], preferred_element_type=F32)


def _ffn_dense_kernel(fc, x_ref, h_ref, mod_ref, w1_ref, w3_ref, w2_ref, o_ref, g_buf):
    g2 = mod_ref[5:6, :]
    o_ref[...] = x_ref[...] + g2 * _swiglu_tile(h_ref, w1_ref, w3_ref, w2_ref, g_buf, fc)


def _ffn_dense(l, x, h2, mod, w1, w3, w2):
    t, d = x.shape
    f = w1.shape[-1]
    s = t // mod.shape[1]
    tm = min(FFN_ROWS, s)
    j = l // 2
    return pl.pallas_call(
        functools.partial(_ffn_dense_kernel, FFN_COLS_DENSE),
        grid=(t // tm,),
        in_specs=[
            pl.BlockSpec((tm, d), lambda i: (i, 0)),
            pl.BlockSpec((tm, d), lambda i: (i, 0)),
            pl.BlockSpec((None, None, 6, d), lambda i: (l, (i * tm) // s, 0, 0)),
            _const_spec((None, d, f), lambda i: (j, 0, 0)),
            _const_spec((None, d, f), lambda i: (j, 0, 0)),
            _const_spec((None, f, d), lambda i: (j, 0, 0)),
        ],
        out_specs=pl.BlockSpec((tm, d), lambda i: (i, 0)),
        out_shape=jax.ShapeDtypeStruct((t, d), F32),
        scratch_shapes=[pltpu.VMEM((tm, f), BF16)],
        compiler_params=pltpu.CompilerParams(
            dimension_semantics=("arbitrary",), vmem_limit_bytes=VMEM_LIMIT_BYTES),
        name="ffn_dense",
    )(x, h2, mod, w1, w3, w2)


def _token_copy(src, src_row, dst, dst_row, sem):
    return pltpu.make_async_copy(src.at[pl.ds(pl.multiple_of(src_row, SUBLANES), SUBLANES), :],
                                 dst.at[pl.ds(pl.multiple_of(dst_row, SUBLANES), SUBLANES), :], sem)


def _ffn_moe_kernel(tm, fc, te_ref, nu_ref, nv_ref, idx_hbm, h_hbm, w1_ref, w3_ref, w2_ref, y_hbm,
                    idx_smem, xbuf, xbf, g_buf, ybuf, idx_sem, in_sem, out_sem):
    i = pl.program_id(0)
    n_used = nu_ref[0]
    slot = lax.rem(i, 2)
    d = xbf.shape[1]

    def idx_base(tile):
        return pl.multiple_of(lax.rem(tile, IDX_SLOTS) * (2 * tm), 2 * tm)

    def idx_copy(tile):
        return pltpu.make_async_copy(idx_hbm.at[tile], idx_smem.at[pl.ds(idx_base(tile), 2 * tm)],
                                     idx_sem.at[lax.rem(tile, IDX_SLOTS)])

    def start_gathers(tile, sl):
        base = idx_base(tile)
        def body(r, carry):
            _token_copy(h_hbm, idx_smem[base + r], xbuf.at[sl], r * SUBLANES, in_sem.at[sl]).start()
            return carry
        lax.fori_loop(0, tm, body, 0, unroll=DMA_UNROLL)

    def wait_gathers(sl):
        pltpu.make_async_copy(h_hbm.at[pl.ds(0, tm * SUBLANES), :], xbuf.at[sl], in_sem.at[sl]).wait()

    def scatter_copy(tile, sl, r):
        return _token_copy(ybuf.at[sl], r * SUBLANES, y_hbm, idx_smem[idx_base(tile) + tm + r],
                           out_sem.at[sl])

    def start_scatters(tile, sl, n_valid):
        @pl.when(n_valid == tm)
        def _():
            def body(r, carry):
                scatter_copy(tile, sl, r).start()
                return carry
            lax.fori_loop(0, tm, body, 0, unroll=DMA_UNROLL)
        @pl.when(n_valid < tm)
        def _():
            def body(r, carry):
                scatter_copy(tile, sl, r).start()
                return carry
            lax.fori_loop(0, n_valid, body, 0)

    def wait_scatters(sl, n_valid):
        @pl.when(n_valid == tm)
        def _():
            pltpu.make_async_copy(ybuf.at[sl], y_hbm.at[pl.ds(0, tm * SUBLANES), :], out_sem.at[sl]).wait()
        @pl.when(n_valid < tm)
        def _():
            def body(r, carry):
                _token_copy(ybuf.at[sl], r * SUBLANES, y_hbm, 0, out_sem.at[sl]).wait()
                return carry
            lax.fori_loop(0, n_valid, body, 0)

    @pl.when(i == 0)
    def _():
        idx_copy(0).start()
        idx_copy(0).wait()
        start_gathers(0, 0)
        @pl.when(n_used > 1)
        def _():
            idx_copy(1).start()

    @pl.when(i < n_used)
    def _():
        wait_gathers(slot)
        @pl.when(i + 2 < n_used)
        def _():
            idx_copy(i + 2).start()
        @pl.when(i + 1 < n_used)
        def _():
            idx_copy(i + 1).wait()
            start_gathers(i + 1, 1 - slot)
        for s8 in range(d // LANES):
            xbf[:, s8 * LANES:(s8 + 1) * LANES] = (
                xbuf[slot, pl.ds(s8, tm, stride=SUBLANES), :].astype(BF16))
        y = _swiglu_tile(xbf, w1_ref, w3_ref, w2_ref, g_buf, fc)
        @pl.when(i >= 2)
        def _():
            wait_scatters(slot, nv_ref[jnp.maximum(i - 2, 0)])
        for s8 in range(d // LANES):
            ybuf[slot, pl.ds(s8, tm, stride=SUBLANES), :] = y[:, s8 * LANES:(s8 + 1) * LANES]
        start_scatters(i, slot, nv_ref[i])
        @pl.when(i == n_used - 1)
        def _():
            @pl.when(i >= 1)
            def _():
                wait_scatters(1 - slot, nv_ref[jnp.maximum(i - 1, 0)])
            wait_scatters(slot, nv_ref[i])


def _ffn_moe(jl, h2t, idx, tile_expert, n_used, n_valid, w1, w3, w2, tm, nt, n_out_rows):
    d, f = w1.shape[-2:]
    grid_spec = pltpu.PrefetchScalarGridSpec(
        num_scalar_prefetch=3,
        grid=(nt,),
        in_specs=[
            pl.BlockSpec(memory_space=pl.ANY),
            pl.BlockSpec(memory_space=pl.ANY),
            _const_spec((None, None, d, f), lambda i, te, nu, nv: (jl, te[i], 0, 0)),
            _const_spec((None, None, d, f), lambda i, te, nu, nv: (jl, te[i], 0, 0)),
            _const_spec((None, None, f, d), lambda i, te, nu, nv: (jl, te[i], 0, 0)),
        ],
        out_specs=pl.BlockSpec(memory_space=pl.ANY),
        scratch_shapes=[
            pltpu.SMEM((IDX_SLOTS * 2 * tm,), jnp.int32),
            pltpu.VMEM((2, tm * SUBLANES, LANES), F32),
            pltpu.VMEM((tm, d), BF16),
            pltpu.VMEM((tm, f), BF16),
            pltpu.VMEM((2, tm * SUBLANES, LANES), F32),
            pltpu.SemaphoreType.DMA((IDX_SLOTS,)), pltpu.SemaphoreType.DMA((2,)),
            pltpu.SemaphoreType.DMA((2,)),
        ],
    )
    return pl.pallas_call(
        functools.partial(_ffn_moe_kernel, tm, FFN_COLS_MOE),
        grid_spec=grid_spec,
        out_shape=jax.ShapeDtypeStruct((n_out_rows, LANES), F32),
        compiler_params=pltpu.CompilerParams(
            dimension_semantics=("arbitrary",), vmem_limit_bytes=VMEM_LIMIT_BYTES,
            disable_bounds_checks=True, has_side_effects=True),
        name="ffn_moe",
    )(tile_expert, n_used, n_valid, idx, h2t, w1, w3, w2)


def _combine_kernel(tc, d, final, x_ref, y0_ref, y1_ref, tw_ref, mod_ref, fg_ref, o_ref):
    g2 = mod_ref[5:6, :]
    tw = tw_ref[...]
    parts = []
    for s8 in range(d // LANES):
        y0 = y0_ref[pl.ds(s8, tc, stride=SUBLANES), :]
        y1 = y1_ref[pl.ds(s8, tc, stride=SUBLANES), :]
        parts.append(tw[:, 0:1] * y0 + tw[:, 1:2] * y1)
    xn = x_ref[...] + g2 * jnp.concatenate(parts, axis=1)
    if final:
        xn = _rms(xn) * fg_ref[...]
    o_ref[...] = xn


def _combine(l, final, x, y2, tw_t, mod, final_g):
    t, d = x.shape
    s = t // mod.shape[1]
    tc = min(COMBINE_ROWS, s)
    nt = t // tc
    return pl.pallas_call(
        functools.partial(_combine_kernel, tc, d, final),
        grid=(nt,),
        in_specs=[
            pl.BlockSpec((tc, d), lambda i: (i, 0)),
            pl.BlockSpec((tc * SUBLANES, LANES), lambda i: (i, 0)),
            pl.BlockSpec((tc * SUBLANES, LANES), lambda i: (nt + i, 0)),
            pl.BlockSpec((tc, 2), lambda i: (i, 0)),
            pl.BlockSpec((None, None, 6, d), lambda i: (l, (i * tc) // s, 0, 0)),
            pl.BlockSpec((1, d), lambda i: (0, 0)),
        ],
        out_specs=pl.BlockSpec((tc, d), lambda i: (i, 0)),
        out_shape=jax.ShapeDtypeStruct((t, d), F32),
        compiler_params=pltpu.CompilerParams(
            dimension_semantics=("arbitrary",), vmem_limit_bytes=VMEM_LIMIT_BYTES),
        name="moe_combine",
    )(x, y2, y2, tw_t, mod, final_g.reshape(1, d))


def _dispatch_plan(ti, tm, nt):
    t = ti.shape[1]
    e_flat = ti.reshape(-1)
    experts = jnp.arange(N_EXPERTS, dtype=jnp.int32)
    counts = jnp.sum((e_flat[:, None] == experts[None, :]).astype(jnp.int32), axis=0)
    order = jnp.argsort(e_flat, stable=True).astype(jnp.int32)
    group_start = jnp.cumsum(counts) - counts
    tiles_e = (counts + tm - 1) // tm
    tile_end = jnp.cumsum(tiles_e)
    tile_start = tile_end - tiles_e
    n_used = tile_end[-1]
    tile_id = jnp.minimum(jnp.arange(nt, dtype=jnp.int32), n_used - 1)
    tile_expert = jnp.minimum(
        jnp.searchsorted(tile_end, tile_id, side="right").astype(jnp.int32), N_EXPERTS - 1)
    first_row = (jnp.arange(nt, dtype=jnp.int32) - tile_start[tile_expert]) * tm
    n_valid = jnp.clip(counts[tile_expert] - first_row, 0, tm)
    n_valid = jnp.where(jnp.arange(nt) < n_used, n_valid, 0).astype(jnp.int32)
    p = first_row[:, None] + jnp.arange(tm, dtype=jnp.int32)[None, :]
    valid = jnp.arange(tm, dtype=jnp.int32)[None, :] < n_valid[:, None]
    a = order[jnp.clip(group_start[tile_expert][:, None] + p, 0, 2 * t - 1)]
    src_row = jnp.where(valid, a % t, 0) * SUBLANES
    dst_row = jnp.where(valid, a, 0) * SUBLANES
    idx = jnp.concatenate([src_row, dst_row], axis=1).astype(jnp.int32)
    return idx, tile_expert, n_used.reshape(1).astype(jnp.int32), n_valid


def kernel(x, c, ada_w, ada_b, w_in, gm_ln_g, gm_ln_b, w_sp, b_sp, w_gm_out, conv_w, conv_b,
           cv_ln_g, cv_ln_b, w_cv_out, w_o, ffn_w1, ffn_w3, ffn_w2, router_w, exp_w1, exp_w3,
           exp_w2, final_g):
    bsz, s, d = x.shape
    depth = ada_w.shape[0]
    t = bsz * s
    assert s % CHUNK == 0 and d % GM_GROUPS == 0 and depth % 2 == 0

    mod = _ada_modulation(c, ada_w, ada_b).reshape(depth, bsz, 6, d)

    win = w_in.astype(BF16)
    wgm, wcv, wo = w_gm_out.astype(BF16), w_cv_out.astype(BF16), w_o.astype(BF16)
    bsp_full = jnp.repeat(jnp.swapaxes(b_sp, 1, 2), d // GM_GROUPS, axis=2)
    zeros = jnp.zeros_like(conv_b)
    vecs = jnp.stack([gm_ln_g, gm_ln_b, cv_ln_g, cv_ln_b, conv_b, zeros, zeros, zeros], axis=1)
    cw = jnp.pad(conv_w, ((0, 0), (0, HALO - CONV_K), (0, 0)))
    cw = jnp.repeat(cw, SUBLANES, axis=1)
    rw_t = jnp.swapaxes(router_w, 1, 2)
    rw_hi = rw_t.astype(BF16)
    rw_t = jnp.concatenate([rw_hi, (rw_t - rw_hi.astype(F32)).astype(BF16)], axis=1)
    fw1, fw3, fw2 = ffn_w1.astype(BF16), ffn_w3.astype(BF16), ffn_w2.astype(BF16)
    ew1, ew3, ew2 = exp_w1.astype(BF16), exp_w3.astype(BF16), exp_w2.astype(BF16)

    tm = min(FFN_ROWS, t)
    nt = (2 * t) // tm + N_EXPERTS

    for l in range(depth):
        moe = l % 2 == 1
        outs = _mixer(l, moe, x, mod, win, w_sp, bsp_full, vecs, cw, wgm, wcv, wo, rw_t)
        if not moe:
            xn, h2 = outs
            x2 = _ffn_dense(l, xn.reshape(t, d), h2.reshape(t, d), mod, fw1, fw3, fw2)
        else:
            xn, h2t, ti, tw = outs
            idx, tile_expert, n_used, n_valid = _dispatch_plan(ti, tm, nt)
            y2 = _ffn_moe(l // 2, h2t, idx, tile_expert, n_used, n_valid, ew1, ew3, ew2, tm, nt,
                          2 * t * SUBLANES)
            x2 = _combine(l, l == depth - 1, xn.reshape(t, d), y2, tw.T, mod, final_g)
        x = x2.reshape(bsz, s, d)
    return x
```

```python
import functools

import jax
import jax.numpy as jnp
from jax import lax
from jax.experimental import pallas as pl
from jax.experimental.pallas import tpu as pltpu

F32 = jnp.float32
BF16 = jnp.bfloat16

EPS = 1e-6
CHUNK = 128
GM_GROUPS = 8
CONV_K = 31
N_EXPERTS = 8
HALO = 32
CONV_ROWS = 64
CONV_LANES = 128
SUBLANES = 8
LANES = 128
VMEM_LIMIT_BYTES = 56 * 1024 * 1024

MIX_ROWS = 512
FFN_ROWS = 512
FFN_COLS_DENSE = 256
FFN_COLS_MOE = 512
COMBINE_ROWS = 256
DMA_UNROLL = 8
IDX_SLOTS = 3


def _const_spec(shape, index_map):
    return pl.BlockSpec(shape, index_map, pipeline_mode=pl.Buffered(1))


def _rms(x):
    return x * lax.rsqrt(jnp.mean(x * x, axis=-1, keepdims=True) + EPS)


def _layer_norm(x, g, b):
    mu = jnp.mean(x, axis=-1, keepdims=True)
    d = x - mu
    var = jnp.mean(d * d, axis=-1, keepdims=True)
    return d * lax.rsqrt(var + EPS) * g + b


def _ada_kernel(c_ref, w_ref, b_ref, o_ref):
    sc = jax.nn.silu(c_ref[...])
    o_ref[...] = jnp.dot(sc, w_ref[...], precision=lax.Precision.HIGHEST,
                         preferred_element_type=F32) + b_ref[...]


def _ada_modulation(c, ada_w, ada_b):
    depth, d, d6 = ada_w.shape
    bsz = c.shape[0]
    nj = d6 // d
    return pl.pallas_call(
        _ada_kernel,
        grid=(depth, nj),
        in_specs=[
            pl.BlockSpec((bsz, d), lambda l, j: (0, 0)),
            pl.BlockSpec((None, d, d), lambda l, j: (l, 0, j)),
            pl.BlockSpec((None, 1, d), lambda l, j: (l, 0, j)),
        ],
        out_specs=pl.BlockSpec((None, bsz, d), lambda l, j: (l, 0, j)),
        out_shape=jax.ShapeDtypeStruct((depth, bsz, d6), F32),
        compiler_params=pltpu.CompilerParams(
            dimension_semantics=("arbitrary", "arbitrary"), vmem_limit_bytes=VMEM_LIMIT_BYTES),
        name="ada_modulation",
    )(c, ada_w, ada_b.reshape(depth, 1, d6))


class _Interleaver:
    def __init__(self):
        self.items = []

    def add(self, name, unit, cost, deps, fn):
        self.items.append((name, unit, cost, tuple(deps), fn))

    def emit(self):
        done, unit_time, pending = {}, {"mxu": 0, "vpu": 0}, list(self.items)
        while pending:
            best = None
            for it in pending:
                if all(dep in done for dep in it[3]):
                    start = max([unit_time[it[1]]] + [done[dep] for dep in it[3]])
                    if best is None or start < best[0]:
                        best = (start, it)
            start, it = best
            it[4]()
            done[it[0]] = unit_time[it[1]] = start + it[2]
            pending.remove(it)


def _mixer_kernel(moe, ts, d, x_ref, mod_ref, win_ref, wsp_ref, bsp_ref, vec_ref, cwb_ref,
                  wgm_ref, wcv_ref, wo_ref, *rest):
    if moe:
        rw_ref, xo_ref, h2_ref, ti_ref, tw_ref = rest[:5]
        scratch = rest[5:]
    else:
        xo_ref, h2_ref = rest[:2]
        scratch = rest[2:]
    (h_buf, hp32, hp_buf, m_buf, z0, z1, z2, z3, zcp, u_buf, v_buf, e_buf, tail_buf, ga_buf, gb_buf,
     ya_buf, yb_buf, wspm) = scratch[:18]
    h2f = scratch[18] if moe else None
    seg = ts // SUBLANES
    assert seg % rb_rows(seg) == 0 and seg >= HALO
    rb = rb_rows(seg)
    qw = 256
    nq = d // qw
    row_blocks = list(range(0, ts, rb))
    first_tile = jnp.logical_and(pl.program_id(0) == 0, pl.program_id(1) == 0)

    @pl.when(first_tile)
    def _():
        row = lax.broadcasted_iota(jnp.int32, (CHUNK, CHUNK), 0)
        col = lax.broadcasted_iota(jnp.int32, (CHUNK, CHUNK), 1)
        for g in range(GM_GROUPS):
            wspm[g] = jnp.where(row >= col, wsp_ref[g], 0.0).astype(BF16)

    @pl.when(pl.program_id(1) == 0)
    def _():
        tail_buf[...] = jnp.zeros(tail_buf.shape, F32)

    sh1, sc1, g1 = mod_ref[0:1, :], mod_ref[1:2, :], mod_ref[2:3, :]
    sh2, sc2 = mod_ref[3:4, :], mod_ref[4:5, :]
    gm_g, gm_b = vec_ref[0:1, :], vec_ref[1:2, :]
    cv_g, cv_b, conv_b = vec_ref[2:3, :], vec_ref[3:4, :], vec_ref[4:5, :]
    gd = d // GM_GROUPS
    off0 = HALO - CONV_K + 1
    conv_groups = CONV_ROWS // SUBLANES
    sched = _Interleaver()

    def qcols(q):
        return slice(q * qw, (q + 1) * qw)

    def perm_rows(r):
        return pl.ds((r % seg) * SUBLANES + r // seg, rb, stride=SUBLANES)

    def norm_item(r):
        def fn():
            xb = x_ref[r:r + rb, :]
            h = _rms(xb) * (1.0 + sc1) + sh1
            h_buf[r:r + rb, :] = h.astype(BF16)
            for s8 in range(d // LANES):
                hp32[s8, perm_rows(r), :] = h[:, s8 * LANES:(s8 + 1) * LANES]
        return fn
    for r in row_blocks:
        sched.add(("norm", r), "vpu", 90, [], norm_item(r))
    all_norm = [("norm", r) for r in row_blocks]

    def perm_pack_item(p0):
        def fn():
            for s8 in range(d // LANES):
                hp_buf[p0:p0 + rb, s8 * LANES:(s8 + 1) * LANES] = hp32[s8, p0:p0 + rb, :].astype(BF16)
        return fn
    for p0 in row_blocks:
        sched.add(("pack", p0), "vpu", 20, all_norm, perm_pack_item(p0))
    all_pack = [("pack", p0) for p0 in row_blocks]

    def in_dot_item(split, q, dst, lhs=h_buf):
        def fn():
            c0 = split * d + q * qw
            dst[:, qcols(q)] = jnp.dot(lhs[...], win_ref[:, c0:c0 + qw], preferred_element_type=F32)
        return fn

    def a_item(q):
        def fn():
            e_buf[HALO * SUBLANES:, qcols(q)] = z0[:, qcols(q)] * jax.nn.sigmoid(z1[:, qcols(q)])
        return fn

    def halo_item(q):
        def fn():
            shape3 = (HALO, SUBLANES, qw)
            cur = e_buf[seg * SUBLANES:, qcols(q)]
            prev = tail_buf[:, qcols(q)]
            first = lax.broadcasted_iota(jnp.int32, shape3, 1) == 0
            halo = jnp.where(first, pltpu.roll(prev.reshape(shape3), 1, axis=1),
                             pltpu.roll(cur.reshape(shape3), 1, axis=1))
            e_buf[0:HALO * SUBLANES, qcols(q)] = halo.reshape(HALO * SUBLANES, qw)
            tail_buf[:, qcols(q)] = cur
        return fn

    def conv_item(jb, q):
        def fn():
            for c0 in range(q * qw, (q + 1) * qw, CONV_LANES):
                cols = slice(c0, c0 + CONV_LANES)
                g0 = jb + off0
                win = e_buf[g0 * SUBLANES:(g0 + conv_groups + CONV_K - 1) * SUBLANES, cols]
                win = win.reshape(conv_groups + CONV_K - 1, SUBLANES, CONV_LANES)
                acc = jnp.broadcast_to(conv_b[:, cols], (conv_groups, SUBLANES, CONV_LANES))
                for k in range(CONV_K):
                    w_k = cwb_ref[k * SUBLANES:(k + 1) * SUBLANES, cols]
                    acc = acc + w_k[None] * win[k:k + conv_groups]
                zcp[c0 // LANES, jb * SUBLANES:(jb + conv_groups) * SUBLANES, :] = (
                    acc.reshape(CONV_ROWS, CONV_LANES))
        return fn

    def cln_item(r):
        def fn():
            y = jnp.concatenate([zcp[s8, perm_rows(r), :] for s8 in range(d // LANES)], axis=1)
            yb_buf[r:r + rb, :] = jax.nn.silu(_layer_norm(y, cv_g, cv_b)).astype(BF16)
        return fn

    def u_item(q):
        def fn():
            u_buf[:, qcols(q)] = jax.nn.gelu(z2[:, qcols(q)]).astype(BF16)
        return fn

    def v_item(r):
        def fn():
            v = jax.nn.gelu(z3[r:r + rb, :])
            v_buf[r:r + rb, :] = _layer_norm(v, gm_g, gm_b).astype(BF16)
        return fn

    def gate_item(q):
        def fn():
            ga_buf[:, qcols(q)] = jax.nn.sigmoid(z0[:, qcols(q)]).astype(BF16)
            gb_buf[:, qcols(q)] = jax.nn.sigmoid(z1[:, qcols(q)]).astype(BF16)
        return fn

    def spatial_item(ci):
        def fn():
            rows = slice(ci * CHUNK, (ci + 1) * CHUNK)
            for g in range(GM_GROUPS):
                cols = slice(g * gd, (g + 1) * gd)
                sv = jnp.dot(wspm[g], v_buf[rows, cols], preferred_element_type=F32) + bsp_ref[:, cols]
                ya_buf[rows, cols] = (u_buf[rows, cols].astype(F32) * sv).astype(BF16)
        return fn

    def out_dot_item(src, w_ref, dst, q, rows=slice(None)):
        def fn():
            dst[rows, qcols(q)] = jnp.dot(src[rows, :], w_ref[:, qcols(q)], preferred_element_type=F32)
        return fn

    def merge_item(q):
        def fn():
            m = (ga_buf[:, qcols(q)].astype(F32) * z2[:, qcols(q)]
                 + gb_buf[:, qcols(q)].astype(F32) * z3[:, qcols(q)])
            m_buf[:, qcols(q)] = m.astype(BF16)
        return fn

    def out_item(r):
        def fn():
            xn = x_ref[r:r + rb, :] + g1 * z0[r:r + rb, :]
            xo_ref[r:r + rb, :] = xn
            h2 = _rms(xn) * (1.0 + sc2) + sh2
            if moe:
                h2f[r:r + rb, :] = h2
                for s8 in range(d // LANES):
                    h2_ref[pl.ds(r * SUBLANES + s8, rb, stride=SUBLANES), :] = (
                        h2[:, s8 * LANES:(s8 + 1) * LANES])
            else:
                h2_ref[r:r + rb, :] = h2.astype(h2_ref.dtype)
        return fn

    conv_blocks = list(range(0, seg, conv_groups))
    chunks = list(range(ts // CHUNK))
    halves = [slice(0, ts // 2), slice(ts // 2, ts)]
    for q in range(nq):
        sched.add(("d_cva", q), "mxu", 512, all_pack, in_dot_item(2, q, z0, hp_buf))
        sched.add(("d_cvg", q), "mxu", 512, all_pack, in_dot_item(3, q, z1, hp_buf))
        sched.add(("a", q), "vpu", 300, [("d_cva", q), ("d_cvg", q)], a_item(q))
        sched.add(("halo", q), "vpu", 60, [("a", q)], halo_item(q))
    for q in range(nq):
        sched.add(("d_u", q), "mxu", 512, all_norm, in_dot_item(0, q, z2))
        sched.add(("u", q), "vpu", 350, [("d_u", q)], u_item(q))
    for q in range(nq):
        sched.add(("d_v", q), "mxu", 512, all_norm, in_dot_item(1, q, z3))
    for r in row_blocks:
        sched.add(("v", r), "vpu", 170, [("d_v", q) for q in range(nq)], v_item(r))
    for q in range(nq):
        sched.add(("d_ga", q), "mxu", 512, all_norm + [("a", q)], in_dot_item(4, q, z0))
        sched.add(("d_gb", q), "mxu", 512, all_norm + [("a", q)], in_dot_item(5, q, z1))
        sched.add(("g", q), "vpu", 550, [("d_ga", q), ("d_gb", q)], gate_item(q))
    for ci in chunks:
        deps = [("v", r) for r in row_blocks if r // CHUNK == ci] + [("u", q) for q in range(nq)]
        sched.add(("sp", ci), "mxu", 600, deps, spatial_item(ci))
    for q in range(nq):
        for jb in conv_blocks:
            deps = [("a", q)] + ([("halo", q)] if jb + off0 < HALO else [])
            sched.add(("conv", jb, q), "vpu", 260, deps, conv_item(jb, q))
    for r in row_blocks:
        j0 = r % seg
        deps = [("conv", jb, q) for jb in conv_blocks if j0 <= jb < j0 + rb for q in range(nq)]
        sched.add(("cln", r), "vpu", 180, deps, cln_item(r))
    all_sp = [("sp", ci) for ci in chunks]
    all_cln = [("cln", r) for r in row_blocks]
    all_v = [("v", r) for r in row_blocks]
    for q in range(nq):
        sched.add(("d_ya", q), "mxu", 512, all_sp, out_dot_item(ya_buf, wgm_ref, z2, q))
        sched.add(("d_yb", q), "mxu", 512, all_cln + all_v, out_dot_item(yb_buf, wcv_ref, z3, q))
        sched.add(("m", q), "vpu", 250, [("d_ya", q), ("d_yb", q), ("g", q)], merge_item(q))
    all_m = [("m", q) for q in range(nq)]
    for hi, rows in enumerate(halves):
        for q in range(nq):
            sched.add(("d_o", hi, q), "mxu", 256, all_m, out_dot_item(m_buf, wo_ref, z0, q, rows))
        for r in row_blocks:
            if rows.start <= r < rows.stop:
                sched.add(("out", r), "vpu", 100, [("d_o", hi, q) for q in range(nq)], out_item(r))
    sched.emit()

    if moe:
        nt_dims = (((1,), (1,)), ((), ()))
        hf = h2f[...]
        hi = hf.astype(BF16)
        lo = (hf - hi.astype(F32)).astype(BF16)
        part = (lax.dot_general(rw_ref[...], hi, nt_dims, preferred_element_type=F32)
                + lax.dot_general(rw_ref[...], lo, nt_dims, preferred_element_type=F32))
        lg = part[0:N_EXPERTS, :] + part[N_EXPERTS:2 * N_EXPERTS, :]
        ls = [lg[e:e + 1, :] for e in range(N_EXPERTS)]
        m1 = functools.reduce(jnp.maximum, ls)
        i1 = jnp.full(m1.shape, N_EXPERTS - 1, jnp.int32)
        for e in range(N_EXPERTS - 2, -1, -1):
            i1 = jnp.where(ls[e] == m1, e, i1)
        ls2 = [jnp.where(i1 == e, -jnp.inf, ls[e]) for e in range(N_EXPERTS)]
        m2 = functools.reduce(jnp.maximum, ls2)
        i2 = jnp.full(m1.shape, N_EXPERTS - 1, jnp.int32)
        for e in range(N_EXPERTS - 2, -1, -1):
            i2 = jnp.where(ls2[e] == m2, e, i2)
        e2 = jnp.exp(m2 - m1)
        den = 1.0 + e2
        ti_ref[0:1, :] = i1
        ti_ref[1:2, :] = i2
        tw_ref[0:1, :] = 1.0 / den
        tw_ref[1:2, :] = e2 / den


def rb_rows(seg):
    return min(32, seg)


def _mixer(l, moe, x, mod, win, wsp, bsp_full, vecs, cw, wgm, wcv, wo, rw_t):
    bsz, s, d = x.shape
    ts = min(MIX_ROWS, s)
    ns = s // ts
    t = bsz * s
    in_specs = [
        pl.BlockSpec((None, ts, d), lambda b, i: (b, i, 0)),
        pl.BlockSpec((None, None, 6, d), lambda b, i: (l, b, 0, 0)),
        _const_spec((None, d, 6 * d), lambda b, i: (l, 0, 0)),
        _const_spec((None, GM_GROUPS, CHUNK, CHUNK), lambda b, i: (l, 0, 0, 0)),
        _const_spec((None, CHUNK, d), lambda b, i: (l, 0, 0)),
        _const_spec((None, 8, d), lambda b, i: (l, 0, 0)),
        _const_spec((None, HALO * SUBLANES, d), lambda b, i: (l, 0, 0)),
        _const_spec((None, d, d), lambda b, i: (l, 0, 0)),
        _const_spec((None, d, d), lambda b, i: (l, 0, 0)),
        _const_spec((None, d, d), lambda b, i: (l, 0, 0)),
    ]
    args = [x, mod, win, wsp, bsp_full, vecs, cw, wgm, wcv, wo]
    tok_rows = d // LANES
    assert tok_rows == SUBLANES
    out_specs = [pl.BlockSpec((None, ts, d), lambda b, i: (b, i, 0))]
    out_shape = [jax.ShapeDtypeStruct((bsz, s, d), F32)]
    if moe:
        out_specs.append(pl.BlockSpec((ts * tok_rows, LANES), lambda b, i: (b * ns + i, 0)))
        out_shape.append(jax.ShapeDtypeStruct((t * tok_rows, LANES), F32))
    else:
        out_specs.append(pl.BlockSpec((None, ts, d), lambda b, i: (b, i, 0)))
        out_shape.append(jax.ShapeDtypeStruct((bsz, s, d), BF16))
    if moe:
        in_specs.append(_const_spec((None, 2 * N_EXPERTS, d), lambda b, i: (l // 2, 0, 0)))
        args.append(rw_t)
        out_specs += [pl.BlockSpec((2, ts), lambda b, i: (0, b * ns + i)),
                      pl.BlockSpec((2, ts), lambda b, i: (0, b * ns + i))]
        out_shape += [jax.ShapeDtypeStruct((2, t), jnp.int32), jax.ShapeDtypeStruct((2, t), F32)]
    scratch = [
        pltpu.VMEM((ts, d), BF16),
        pltpu.VMEM((d // LANES, ts, LANES), F32),
        pltpu.VMEM((ts, d), BF16),
        pltpu.VMEM((ts, d), BF16),
        pltpu.VMEM((ts, d), F32),
        pltpu.VMEM((ts, d), F32),
        pltpu.VMEM((ts, d), F32),
        pltpu.VMEM((ts, d), F32),
        pltpu.VMEM((d // LANES, ts, LANES), F32),
        pltpu.VMEM((ts, d), BF16),
        pltpu.VMEM((ts, d), BF16),
        pltpu.VMEM((ts + HALO * SUBLANES, d), F32),
        pltpu.VMEM((HALO * SUBLANES, d), F32),
        pltpu.VMEM((ts, d), BF16),
        pltpu.VMEM((ts, d), BF16),
        pltpu.VMEM((ts, d), BF16),
        pltpu.VMEM((ts, d), BF16),
        pltpu.VMEM((GM_GROUPS, CHUNK, CHUNK), BF16),
    ]
    if moe:
        scratch.append(pltpu.VMEM((ts, d), F32))
    return pl.pallas_call(
        functools.partial(_mixer_kernel, moe, ts, d),
        grid=(bsz, ns),
        in_specs=in_specs,
        out_specs=out_specs,
        out_shape=out_shape,
        scratch_shapes=scratch,
        compiler_params=pltpu.CompilerParams(
            dimension_semantics=("arbitrary", "arbitrary"), vmem_limit_bytes=VMEM_LIMIT_BYTES),
        name="mixer_moe" if moe else "mixer_dense",
    )(*args)


def _swiglu_tile(x_ref, w1_ref, w3_ref, w2_ref, g_buf, fc):
    f = w1_ref.shape[-1]
    x = x_ref[...]
    for c0 in range(0, f, fc):
        cols = slice(c0, c0 + fc)
        h1 = jnp.dot(x, w1_ref[:, cols], preferred_element_type=F32)
        h3 = jnp.dot(x, w3_ref[:, cols], preferred_element_type=F32)
        g_buf[:, cols] = (jax.nn.silu(h1) * h3).astype(BF16)
    return jnp.dot(g_buf[...], w2_ref[...], preferred_element_type=F32)


def _ffn_dense_kernel(fc, x_ref, h_ref, mod_ref, w1_ref, w3_ref, w2_ref, o_ref, g_buf):
    g2 = mod_ref[5:6, :]
    o_ref[...] = x_ref[...] + g2 * _swiglu_tile(h_ref, w1_ref, w3_ref, w2_ref, g_buf, fc)


def _ffn_dense(l, x, h2, mod, w1, w3, w2):
    t, d = x.shape
    f = w1.shape[-1]
    s = t // mod.shape[1]
    tm = min(FFN_ROWS, s)
    j = l // 2
    return pl.pallas_call(
        functools.partial(_ffn_dense_kernel, FFN_COLS_DENSE),
        grid=(t // tm,),
        in_specs=[
            pl.BlockSpec((tm, d), lambda i: (i, 0)),
            pl.BlockSpec((tm, d), lambda i: (i, 0)),
            pl.BlockSpec((None, None, 6, d), lambda i: (l, (i * tm) // s, 0, 0)),
            _const_spec((None, d, f), lambda i: (j, 0, 0)),
            _const_spec((None, d, f), lambda i: (j, 0, 0)),
            _const_spec((None, f, d), lambda i: (j, 0, 0)),
        ],
        out_specs=pl.BlockSpec((tm, d), lambda i: (i, 0)),
        out_shape=jax.ShapeDtypeStruct((t, d), F32),
        scratch_shapes=[pltpu.VMEM((tm, f), BF16)],
        compiler_params=pltpu.CompilerParams(
            dimension_semantics=("arbitrary",), vmem_limit_bytes=VMEM_LIMIT_BYTES),
        name="ffn_dense",
    )(x, h2, mod, w1, w3, w2)


def _token_copy(src, src_row, dst, dst_row, sem):
    return pltpu.make_async_copy(src.at[pl.ds(pl.multiple_of(src_row, SUBLANES), SUBLANES), :],
                                 dst.at[pl.ds(pl.multiple_of(dst_row, SUBLANES), SUBLANES), :], sem)


def _ffn_moe_kernel(tm, fc, te_ref, nu_ref, nv_ref, idx_hbm, h_hbm, w1_ref, w3_ref, w2_ref, y_hbm,
                    idx_smem, xbuf, xbf, g_buf, ybuf, idx_sem, in_sem, out_sem):
    i = pl.program_id(0)
    n_used = nu_ref[0]
    slot = lax.rem(i, 2)
    d = xbf.shape[1]

    def idx_base(tile):
        return pl.multiple_of(lax.rem(tile, IDX_SLOTS) * (2 * tm), 2 * tm)

    def idx_copy(tile):
        return pltpu.make_async_copy(idx_hbm.at[tile], idx_smem.at[pl.ds(idx_base(tile), 2 * tm)],
                                     idx_sem.at[lax.rem(tile, IDX_SLOTS)])

    def start_gathers(tile, sl):
        base = idx_base(tile)
        def body(r, carry):
            _token_copy(h_hbm, idx_smem[base + r], xbuf.at[sl], r * SUBLANES, in_sem.at[sl]).start()
            return carry
        lax.fori_loop(0, tm, body, 0, unroll=DMA_UNROLL)

    def wait_gathers(sl):
        pltpu.make_async_copy(h_hbm.at[pl.ds(0, tm * SUBLANES), :], xbuf.at[sl], in_sem.at[sl]).wait()

    def scatter_copy(tile, sl, r):
        return _token_copy(ybuf.at[sl], r * SUBLANES, y_hbm, idx_smem[idx_base(tile) + tm + r],
                           out_sem.at[sl])

    def start_scatters(tile, sl, n_valid):
        @pl.when(n_valid == tm)
        def _():
            def body(r, carry):
                scatter_copy(tile, sl, r).start()
                return carry
            lax.fori_loop(0, tm, body, 0, unroll=DMA_UNROLL)
        @pl.when(n_valid < tm)
        def _():
            def body(r, carry):
                scatter_copy(tile, sl, r).start()
                return carry
            lax.fori_loop(0, n_valid, body, 0)

    def wait_scatters(sl, n_valid):
        @pl.when(n_valid == tm)
        def _():
            pltpu.make_async_copy(ybuf.at[sl], y_hbm.at[pl.ds(0, tm * SUBLANES), :], out_sem.at[sl]).wait()
        @pl.when(n_valid < tm)
        def _():
            def body(r, carry):
                _token_copy(ybuf.at[sl], r * SUBLANES, y_hbm, 0, out_sem.at[sl]).wait()
                return carry
            lax.fori_loop(0, n_valid, body, 0)

    @pl.when(i == 0)
    def _():
        idx_copy(0).start()
        idx_copy(0).wait()
        start_gathers(0, 0)
        idx_copy(1).start()

    @pl.when(i < n_used)
    def _():
        wait_gathers(slot)
        @pl.when(i + 2 <= n_used)
        def _():
            idx_copy(i + 2).start()
        idx_copy(i + 1).wait()
        for s8 in range(d // LANES):
            xbf[:, s8 * LANES:(s8 + 1) * LANES] = (
                xbuf[slot, pl.ds(s8, tm, stride=SUBLANES), :].astype(BF16))
        base = idx_base(i + 1)
        for r in range(tm):
            _token_copy(h_hbm, idx_smem[base + r], xbuf.at[1 - slot], r * SUBLANES,
                        in_sem.at[1 - slot]).start()
        y = _swiglu_tile(xbf, w1_ref, w3_ref, w2_ref, g_buf, fc)
        @pl.when(i >= 2)
        def _():
            wait_scatters(slot, nv_ref[jnp.maximum(i - 2, 0)])
        for s8 in range(d // LANES):
            ybuf[slot, pl.ds(s8, tm, stride=SUBLANES), :] = y[:, s8 * LANES:(s8 + 1) * LANES]
        start_scatters(i, slot, nv_ref[i])
        @pl.when(i == n_used - 1)
        def _():
            wait_gathers(1 - slot)
            @pl.when(i >= 1)
            def _():
                wait_scatters(1 - slot, nv_ref[jnp.maximum(i - 1, 0)])
            wait_scatters(slot, nv_ref[i])


def _ffn_moe(jl, h2t, idx, tile_expert, n_used, n_valid, w1, w3, w2, tm, nt, n_out_rows):
    d, f = w1.shape[-2:]
    grid_spec = pltpu.PrefetchScalarGridSpec(
        num_scalar_prefetch=3,
        grid=(nt,),
        in_specs=[
            pl.BlockSpec(memory_space=pl.ANY),
            pl.BlockSpec(memory_space=pl.ANY),
            _const_spec((None, None, d, f), lambda i, te, nu, nv: (jl, te[i], 0, 0)),
            _const_spec((None, None, d, f), lambda i, te, nu, nv: (jl, te[i], 0, 0)),
            _const_spec((None, None, f, d), lambda i, te, nu, nv: (jl, te[i], 0, 0)),
        ],
        out_specs=pl.BlockSpec(memory_space=pl.ANY),
        scratch_shapes=[
            pltpu.SMEM((IDX_SLOTS * 2 * tm,), jnp.int32),
            pltpu.VMEM((2, tm * SUBLANES, LANES), F32),
            pltpu.VMEM((tm, d), BF16),
            pltpu.VMEM((tm, f), BF16),
            pltpu.VMEM((2, tm * SUBLANES, LANES), F32),
            pltpu.SemaphoreType.DMA((IDX_SLOTS,)), pltpu.SemaphoreType.DMA((2,)),
            pltpu.SemaphoreType.DMA((2,)),
        ],
    )
    return pl.pallas_call(
        functools.partial(_ffn_moe_kernel, tm, FFN_COLS_MOE),
        grid_spec=grid_spec,
        out_shape=jax.ShapeDtypeStruct((n_out_rows, LANES), F32),
        compiler_params=pltpu.CompilerParams(
            dimension_semantics=("arbitrary",), vmem_limit_bytes=VMEM_LIMIT_BYTES,
            disable_bounds_checks=True, has_side_effects=True),
        name="ffn_moe",
    )(tile_expert, n_used, n_valid, idx, h2t, w1, w3, w2)


def _combine_kernel(tc, d, final, x_ref, y0_ref, y1_ref, tw_ref, mod_ref, fg_ref, o_ref):
    g2 = mod_ref[5:6, :]
    tw = tw_ref[...]
    parts = []
    for s8 in range(d // LANES):
        y0 = y0_ref[pl.ds(s8, tc, stride=SUBLANES), :]
        y1 = y1_ref[pl.ds(s8, tc, stride=SUBLANES), :]
        parts.append(tw[:, 0:1] * y0 + tw[:, 1:2] * y1)
    xn = x_ref[...] + g2 * jnp.concatenate(parts, axis=1)
    if final:
        xn = _rms(xn) * fg_ref[...]
    o_ref[...] = xn


def _combine(l, final, x, y2, tw_t, mod, final_g):
    t, d = x.shape
    s = t // mod.shape[1]
    tc = min(COMBINE_ROWS, s)
    nt = t // tc
    return pl.pallas_call(
        functools.partial(_combine_kernel, tc, d, final),
        grid=(nt,),
        in_specs=[
            pl.BlockSpec((tc, d), lambda i: (i, 0)),
            pl.BlockSpec((tc * SUBLANES, LANES), lambda i: (i, 0)),
            pl.BlockSpec((tc * SUBLANES, LANES), lambda i: (nt + i, 0)),
            pl.BlockSpec((tc, 2), lambda i: (i, 0)),
            pl.BlockSpec((None, None, 6, d), lambda i: (l, (i * tc) // s, 0, 0)),
            pl.BlockSpec((1, d), lambda i: (0, 0)),
        ],
        out_specs=pl.BlockSpec((tc, d), lambda i: (i, 0)),
        out_shape=jax.ShapeDtypeStruct((t, d), F32),
        compiler_params=pltpu.CompilerParams(
            dimension_semantics=("arbitrary",), vmem_limit_bytes=VMEM_LIMIT_BYTES),
        name="moe_combine",
    )(x, y2, y2, tw_t, mod, final_g.reshape(1, d))


def _dispatch_plan(ti, tm, nt):
    t = ti.shape[1]
    e_flat = ti.reshape(-1)
    experts = jnp.arange(N_EXPERTS, dtype=jnp.int32)
    counts = jnp.sum((e_flat[:, None] == experts[None, :]).astype(jnp.int32), axis=0)
    order = jnp.argsort(e_flat, stable=True).astype(jnp.int32)
    group_start = jnp.cumsum(counts) - counts
    tiles_e = (counts + tm - 1) // tm
    tile_end = jnp.cumsum(tiles_e)
    tile_start = tile_end - tiles_e
    n_used = tile_end[-1]
    tile_id = jnp.minimum(jnp.arange(nt, dtype=jnp.int32), n_used - 1)
    tile_expert = jnp.minimum(
        jnp.searchsorted(tile_end, tile_id, side="right").astype(jnp.int32), N_EXPERTS - 1)
    first_row = (jnp.arange(nt, dtype=jnp.int32) - tile_start[tile_expert]) * tm
    n_valid = jnp.clip(counts[tile_expert] - first_row, 0, tm)
    n_valid = jnp.where(jnp.arange(nt) < n_used, n_valid, 0).astype(jnp.int32)
    p = first_row[:, None] + jnp.arange(tm, dtype=jnp.int32)[None, :]
    valid = jnp.arange(tm, dtype=jnp.int32)[None, :] < n_valid[:, None]
    a = order[jnp.clip(group_start[tile_expert][:, None] + p, 0, 2 * t - 1)]
    src_row = jnp.where(valid, a % t, 0) * SUBLANES
    dst_row = jnp.where(valid, a, 0) * SUBLANES
    idx = jnp.concatenate([src_row, dst_row], axis=1).astype(jnp.int32)
    return idx, tile_expert, n_used.reshape(1).astype(jnp.int32), n_valid


def kernel(x, c, ada_w, ada_b, w_in, gm_ln_g, gm_ln_b, w_sp, b_sp, w_gm_out, conv_w, conv_b,
           cv_ln_g, cv_ln_b, w_cv_out, w_o, ffn_w1, ffn_w3, ffn_w2, router_w, exp_w1, exp_w3,
           exp_w2, final_g):
    bsz, s, d = x.shape
    depth = ada_w.shape[0]
    t = bsz * s
    assert s % CHUNK == 0 and d % GM_GROUPS == 0 and depth % 2 == 0

    mod = _ada_modulation(c, ada_w, ada_b).reshape(depth, bsz, 6, d)

    win = w_in.astype(BF16)
    wgm, wcv, wo = w_gm_out.astype(BF16), w_cv_out.astype(BF16), w_o.astype(BF16)
    bsp_full = jnp.repeat(jnp.swapaxes(b_sp, 1, 2), d // GM_GROUPS, axis=2)
    zeros = jnp.zeros_like(conv_b)
    vecs = jnp.stack([gm_ln_g, gm_ln_b, cv_ln_g, cv_ln_b, conv_b, zeros, zeros, zeros], axis=1)
    cw = jnp.pad(conv_w, ((0, 0), (0, HALO - CONV_K), (0, 0)))
    cw = jnp.repeat(cw, SUBLANES, axis=1)
    rw_t = jnp.swapaxes(router_w, 1, 2)
    rw_hi = rw_t.astype(BF16)
    rw_t = jnp.concatenate([rw_hi, (rw_t - rw_hi.astype(F32)).astype(BF16)], axis=1)
    fw1, fw3, fw2 = ffn_w1.astype(BF16), ffn_w3.astype(BF16), ffn_w2.astype(BF16)
    ew1, ew3, ew2 = exp_w1.astype(BF16), exp_w3.astype(BF16), exp_w2.astype(BF16)

    tm = min(FFN_ROWS, t)
    nt = (2 * t) // tm + N_EXPERTS

    for l in range(depth):
        moe = l % 2 == 1
        outs = _mixer(l, moe, x, mod, win, w_sp, bsp_full, vecs, cw, wgm, wcv, wo, rw_t)
        if not moe:
            xn, h2 = outs
            x2 = _ffn_dense(l, xn.reshape(t, d), h2.reshape(t, d), mod, fw1, fw3, fw2)
        else:
            xn, h2t, ti, tw = outs
            idx, tile_expert, n_used, n_valid = _dispatch_plan(ti, tm, nt)
            y2 = _ffn_moe(l // 2, h2t, idx, tile_expert, n_used, n_valid, ew1, ew3, ew2, tm, nt,
                          2 * t * SUBLANES)
            x2 = _combine(l, l == depth - 1, xn.reshape(t, d), y2, tw.T, mod, final_g)
        x = x2.reshape(bsz, s, d)
    return x
```

```python
import functools

import jax
import jax.numpy as jnp
from jax import lax
from jax.experimental import pallas as pl
from jax.experimental.pallas import tpu as pltpu

F32 = jnp.float32
BF16 = jnp.bfloat16

EPS = 1e-6
CHUNK = 128
GM_GROUPS = 8
CONV_K = 31
N_EXPERTS = 8
HALO = 32
CONV_ROWS = 64
CONV_LANES = 128
SUBLANES = 8
LANES = 128
VMEM_LIMIT_BYTES = 56 * 1024 * 1024

MIX_ROWS = 512
FFN_ROWS = 512
FFN_COLS_DENSE = 256
FFN_COLS_MOE = 512
COMBINE_ROWS = 256
DMA_UNROLL = 8
IDX_SLOTS = 4


def _const_spec(shape, index_map):
    return pl.BlockSpec(shape, index_map, pipeline_mode=pl.Buffered(1))


def _rms(x):
    return x * lax.rsqrt(jnp.mean(x * x, axis=-1, keepdims=True) + EPS)


def _layer_norm(x, g, b):
    mu = jnp.mean(x, axis=-1, keepdims=True)
    d = x - mu
    var = jnp.mean(d * d, axis=-1, keepdims=True)
    return d * lax.rsqrt(var + EPS) * g + b


def _ada_kernel(c_ref, w_ref, b_ref, o_ref):
    sc = jax.nn.silu(c_ref[...])
    o_ref[...] = jnp.dot(sc, w_ref[...], precision=lax.Precision.HIGHEST,
                         preferred_element_type=F32) + b_ref[...]


def _ada_modulation(c, ada_w, ada_b):
    depth, d, d6 = ada_w.shape
    bsz = c.shape[0]
    nj = d6 // d
    return pl.pallas_call(
        _ada_kernel,
        grid=(depth, nj),
        in_specs=[
            pl.BlockSpec((bsz, d), lambda l, j: (0, 0)),
            pl.BlockSpec((None, d, d), lambda l, j: (l, 0, j)),
            pl.BlockSpec((None, 1, d), lambda l, j: (l, 0, j)),
        ],
        out_specs=pl.BlockSpec((None, bsz, d), lambda l, j: (l, 0, j)),
        out_shape=jax.ShapeDtypeStruct((depth, bsz, d6), F32),
        compiler_params=pltpu.CompilerParams(
            dimension_semantics=("arbitrary", "arbitrary"), vmem_limit_bytes=VMEM_LIMIT_BYTES),
        name="ada_modulation",
    )(c, ada_w, ada_b.reshape(depth, 1, d6))


class _Interleaver:
    def __init__(self):
        self.items = []

    def add(self, name, unit, cost, deps, fn):
        self.items.append((name, unit, cost, tuple(deps), fn))

    def emit(self):
        done, unit_time, pending = {}, {"mxu": 0, "vpu": 0}, list(self.items)
        while pending:
            best = None
            for it in pending:
                if all(dep in done for dep in it[3]):
                    start = max([unit_time[it[1]]] + [done[dep] for dep in it[3]])
                    if best is None or start < best[0]:
                        best = (start, it)
            start, it = best
            it[4]()
            done[it[0]] = unit_time[it[1]] = start + it[2]
            pending.remove(it)


def _mixer_kernel(moe, ts, d, x_ref, mod_ref, win_ref, wsp_ref, bsp_ref, vec_ref, cwb_ref,
                  wgm_ref, wcv_ref, wo_ref, *rest):
    if moe:
        rw_ref, xo_ref, h2_ref, ti_ref, tw_ref = rest[:5]
        scratch = rest[5:]
    else:
        xo_ref, h2_ref = rest[:2]
        scratch = rest[2:]
    (h_buf, hp32, hp_buf, m_buf, z0, z1, z2, z3, zcp, u_buf, v_buf, e_buf, tail_buf, ga_buf, gb_buf,
     ya_buf, yb_buf, wspm) = scratch[:18]
    h2f = scratch[18] if moe else None
    seg = ts // SUBLANES
    assert seg % rb_rows(seg) == 0 and seg >= HALO
    rb = rb_rows(seg)
    qw = 256
    nq = d // qw
    row_blocks = list(range(0, ts, rb))
    first_tile = jnp.logical_and(pl.program_id(0) == 0, pl.program_id(1) == 0)

    @pl.when(first_tile)
    def _():
        row = lax.broadcasted_iota(jnp.int32, (CHUNK, CHUNK), 0)
        col = lax.broadcasted_iota(jnp.int32, (CHUNK, CHUNK), 1)
        for g in range(GM_GROUPS):
            wspm[g] = jnp.where(row >= col, wsp_ref[g], 0.0).astype(BF16)

    @pl.when(pl.program_id(1) == 0)
    def _():
        tail_buf[...] = jnp.zeros(tail_buf.shape, F32)

    sh1, sc1, g1 = mod_ref[0:1, :], mod_ref[1:2, :], mod_ref[2:3, :]
    sh2, sc2 = mod_ref[3:4, :], mod_ref[4:5, :]
    gm_g, gm_b = vec_ref[0:1, :], vec_ref[1:2, :]
    cv_g, cv_b, conv_b = vec_ref[2:3, :], vec_ref[3:4, :], vec_ref[4:5, :]
    gd = d // GM_GROUPS
    off0 = HALO - CONV_K + 1
    conv_groups = CONV_ROWS // SUBLANES
    sched = _Interleaver()

    def qcols(q):
        return slice(q * qw, (q + 1) * qw)

    def perm_rows(r):
        return pl.ds((r % seg) * SUBLANES + r // seg, rb, stride=SUBLANES)

    def norm_item(r):
        def fn():
            xb = x_ref[r:r + rb, :]
            h = _rms(xb) * (1.0 + sc1) + sh1
            h_buf[r:r + rb, :] = h.astype(BF16)
            for s8 in range(d // LANES):
                hp32[s8, perm_rows(r), :] = h[:, s8 * LANES:(s8 + 1) * LANES]
        return fn
    for r in row_blocks:
        sched.add(("norm", r), "vpu", 90, [], norm_item(r))
    all_norm = [("norm", r) for r in row_blocks]

    def perm_pack_item(p0):
        def fn():
            for s8 in range(d // LANES):
                hp_buf[p0:p0 + rb, s8 * LANES:(s8 + 1) * LANES] = hp32[s8, p0:p0 + rb, :].astype(BF16)
        return fn
    for p0 in row_blocks:
        sched.add(("pack", p0), "vpu", 20, all_norm, perm_pack_item(p0))
    all_pack = [("pack", p0) for p0 in row_blocks]

    def in_dot_item(split, q, dst, lhs=h_buf):
        def fn():
            c0 = split * d + q * qw
            dst[:, qcols(q)] = jnp.dot(lhs[...], win_ref[:, c0:c0 + qw], preferred_element_type=F32)
        return fn

    def a_item(q):
        def fn():
            e_buf[HALO * SUBLANES:, qcols(q)] = z0[:, qcols(q)] * jax.nn.sigmoid(z1[:, qcols(q)])
        return fn

    def halo_item(q):
        def fn():
            shape3 = (HALO, SUBLANES, qw)
            cur = e_buf[seg * SUBLANES:, qcols(q)]
            prev = tail_buf[:, qcols(q)]
            first = lax.broadcasted_iota(jnp.int32, shape3, 1) == 0
            halo = jnp.where(first, pltpu.roll(prev.reshape(shape3), 1, axis=1),
                             pltpu.roll(cur.reshape(shape3), 1, axis=1))
            e_buf[0:HALO * SUBLANES, qcols(q)] = halo.reshape(HALO * SUBLANES, qw)
            tail_buf[:, qcols(q)] = cur
        return fn

    def conv_item(jb, q):
        def fn():
            for c0 in range(q * qw, (q + 1) * qw, CONV_LANES):
                cols = slice(c0, c0 + CONV_LANES)
                g0 = jb + off0
                win = e_buf[g0 * SUBLANES:(g0 + conv_groups + CONV_K - 1) * SUBLANES, cols]
                win = win.reshape(conv_groups + CONV_K - 1, SUBLANES, CONV_LANES)
                acc = jnp.broadcast_to(conv_b[:, cols], (conv_groups, SUBLANES, CONV_LANES))
                for k in range(CONV_K):
                    w_k = cwb_ref[k * SUBLANES:(k + 1) * SUBLANES, cols]
                    acc = acc + w_k[None] * win[k:k + conv_groups]
                zcp[c0 // LANES, jb * SUBLANES:(jb + conv_groups) * SUBLANES, :] = (
                    acc.reshape(CONV_ROWS, CONV_LANES))
        return fn

    def cln_item(r):
        def fn():
            y = jnp.concatenate([zcp[s8, perm_rows(r), :] for s8 in range(d // LANES)], axis=1)
            yb_buf[r:r + rb, :] = jax.nn.silu(_layer_norm(y, cv_g, cv_b)).astype(BF16)
        return fn

    def u_item(q):
        def fn():
            u_buf[:, qcols(q)] = jax.nn.gelu(z2[:, qcols(q)]).astype(BF16)
        return fn

    def v_item(r):
        def fn():
            v = jax.nn.gelu(z3[r:r + rb, :])
            v_buf[r:r + rb, :] = _layer_norm(v, gm_g, gm_b).astype(BF16)
        return fn

    def gate_item(q):
        def fn():
            ga_buf[:, qcols(q)] = jax.nn.sigmoid(z0[:, qcols(q)]).astype(BF16)
            gb_buf[:, qcols(q)] = jax.nn.sigmoid(z1[:, qcols(q)]).astype(BF16)
        return fn

    def spatial_item(ci):
        def fn():
            rows = slice(ci * CHUNK, (ci + 1) * CHUNK)
            for g in range(GM_GROUPS):
                cols = slice(g * gd, (g + 1) * gd)
                sv = jnp.dot(wspm[g], v_buf[rows, cols], preferred_element_type=F32) + bsp_ref[:, cols]
                ya_buf[rows, cols] = (u_buf[rows, cols].astype(F32) * sv).astype(BF16)
        return fn

    def out_dot_item(src, w_ref, dst, q, rows=slice(None)):
        def fn():
            dst[rows, qcols(q)] = jnp.dot(src[rows, :], w_ref[:, qcols(q)], preferred_element_type=F32)
        return fn

    def merge_item(q):
        def fn():
            m = (ga_buf[:, qcols(q)].astype(F32) * z2[:, qcols(q)]
                 + gb_buf[:, qcols(q)].astype(F32) * z3[:, qcols(q)])
            m_buf[:, qcols(q)] = m.astype(BF16)
        return fn

    def out_item(r):
        def fn():
            xn = x_ref[r:r + rb, :] + g1 * z0[r:r + rb, :]
            xo_ref[r:r + rb, :] = xn
            h2 = _rms(xn) * (1.0 + sc2) + sh2
            if moe:
                h2f[r:r + rb, :] = h2
                for s8 in range(d // LANES):
                    h2_ref[pl.ds(r * SUBLANES + s8, rb, stride=SUBLANES), :] = (
                        h2[:, s8 * LANES:(s8 + 1) * LANES])
            else:
                h2_ref[r:r + rb, :] = h2.astype(h2_ref.dtype)
        return fn

    conv_blocks = list(range(0, seg, conv_groups))
    chunks = list(range(ts // CHUNK))
    halves = [slice(0, ts // 2), slice(ts // 2, ts)]
    for q in range(nq):
        sched.add(("d_cva", q), "mxu", 512, all_pack, in_dot_item(2, q, z0, hp_buf))
        sched.add(("d_cvg", q), "mxu", 512, all_pack, in_dot_item(3, q, z1, hp_buf))
        sched.add(("a", q), "vpu", 300, [("d_cva", q), ("d_cvg", q)], a_item(q))
        sched.add(("halo", q), "vpu", 60, [("a", q)], halo_item(q))
    for q in range(nq):
        sched.add(("d_u", q), "mxu", 512, all_norm, in_dot_item(0, q, z2))
        sched.add(("u", q), "vpu", 350, [("d_u", q)], u_item(q))
    for q in range(nq):
        sched.add(("d_v", q), "mxu", 512, all_norm, in_dot_item(1, q, z3))
    for r in row_blocks:
        sched.add(("v", r), "vpu", 170, [("d_v", q) for q in range(nq)], v_item(r))
    for q in range(nq):
        sched.add(("d_ga", q), "mxu", 512, all_norm + [("a", q)], in_dot_item(4, q, z0))
        sched.add(("d_gb", q), "mxu", 512, all_norm + [("a", q)], in_dot_item(5, q, z1))
        sched.add(("g", q), "vpu", 550, [("d_ga", q), ("d_gb", q)], gate_item(q))
    for ci in chunks:
        deps = [("v", r) for r in row_blocks if r // CHUNK == ci] + [("u", q) for q in range(nq)]
        sched.add(("sp", ci), "mxu", 600, deps, spatial_item(ci))
    for q in range(nq):
        for jb in conv_blocks:
            deps = [("a", q)] + ([("halo", q)] if jb + off0 < HALO else [])
            sched.add(("conv", jb, q), "vpu", 260, deps, conv_item(jb, q))
    for r in row_blocks:
        j0 = r % seg
        deps = [("conv", jb, q) for jb in conv_blocks if j0 <= jb < j0 + rb for q in range(nq)]
        sched.add(("cln", r), "vpu", 180, deps, cln_item(r))
    all_sp = [("sp", ci) for ci in chunks]
    all_cln = [("cln", r) for r in row_blocks]
    all_v = [("v", r) for r in row_blocks]
    for q in range(nq):
        sched.add(("d_ya", q), "mxu", 512, all_sp, out_dot_item(ya_buf, wgm_ref, z2, q))
        sched.add(("d_yb", q), "mxu", 512, all_cln + all_v, out_dot_item(yb_buf, wcv_ref, z3, q))
        sched.add(("m", q), "vpu", 250, [("d_ya", q), ("d_yb", q), ("g", q)], merge_item(q))
    all_m = [("m", q) for q in range(nq)]
    for hi, rows in enumerate(halves):
        for q in range(nq):
            sched.add(("d_o", hi, q), "mxu", 256, all_m, out_dot_item(m_buf, wo_ref, z0, q, rows))
        for r in row_blocks:
            if rows.start <= r < rows.stop:
                sched.add(("out", r), "vpu", 100, [("d_o", hi, q) for q in range(nq)], out_item(r))
    sched.emit()

    if moe:
        nt_dims = (((1,), (1,)), ((), ()))
        hf = h2f[...]
        hi = hf.astype(BF16)
        lo = (hf - hi.astype(F32)).astype(BF16)
        part = (lax.dot_general(rw_ref[...], hi, nt_dims, preferred_element_type=F32)
                + lax.dot_general(rw_ref[...], lo, nt_dims, preferred_element_type=F32))
        lg = part[0:N_EXPERTS, :] + part[N_EXPERTS:2 * N_EXPERTS, :]
        ls = [lg[e:e + 1, :] for e in range(N_EXPERTS)]
        m1 = functools.reduce(jnp.maximum, ls)
        i1 = jnp.full(m1.shape, N_EXPERTS - 1, jnp.int32)
        for e in range(N_EXPERTS - 2, -1, -1):
            i1 = jnp.where(ls[e] == m1, e, i1)
        ls2 = [jnp.where(i1 == e, -jnp.inf, ls[e]) for e in range(N_EXPERTS)]
        m2 = functools.reduce(jnp.maximum, ls2)
        i2 = jnp.full(m1.shape, N_EXPERTS - 1, jnp.int32)
        for e in range(N_EXPERTS - 2, -1, -1):
            i2 = jnp.where(ls2[e] == m2, e, i2)
        e2 = jnp.exp(m2 - m1)
        den = 1.0 + e2
        ti_ref[0:1, :] = i1
        ti_ref[1:2, :] = i2
        tw_ref[0:1, :] = 1.0 / den
        tw_ref[1:2, :] = e2 / den


def rb_rows(seg):
    return min(32, seg)


def _mixer(l, moe, x, mod, win, wsp, bsp_full, vecs, cw, wgm, wcv, wo, rw_t):
    bsz, s, d = x.shape
    ts = min(MIX_ROWS, s)
    ns = s // ts
    t = bsz * s
    in_specs = [
        pl.BlockSpec((None, ts, d), lambda b, i: (b, i, 0)),
        pl.BlockSpec((None, None, 6, d), lambda b, i: (l, b, 0, 0)),
        _const_spec((None, d, 6 * d), lambda b, i: (l, 0, 0)),
        _const_spec((None, GM_GROUPS, CHUNK, CHUNK), lambda b, i: (l, 0, 0, 0)),
        _const_spec((None, CHUNK, d), lambda b, i: (l, 0, 0)),
        _const_spec((None, 8, d), lambda b, i: (l, 0, 0)),
        _const_spec((None, HALO * SUBLANES, d), lambda b, i: (l, 0, 0)),
        _const_spec((None, d, d), lambda b, i: (l, 0, 0)),
        _const_spec((None, d, d), lambda b, i: (l, 0, 0)),
        _const_spec((None, d, d), lambda b, i: (l, 0, 0)),
    ]
    args = [x, mod, win, wsp, bsp_full, vecs, cw, wgm, wcv, wo]
    tok_rows = d // LANES
    assert tok_rows == SUBLANES
    out_specs = [pl.BlockSpec((None, ts, d), lambda b, i: (b, i, 0))]
    out_shape = [jax.ShapeDtypeStruct((bsz, s, d), F32)]
    if moe:
        out_specs.append(pl.BlockSpec((ts * tok_rows, LANES), lambda b, i: (b * ns + i, 0)))
        out_shape.append(jax.ShapeDtypeStruct((t * tok_rows, LANES), F32))
    else:
        out_specs.append(pl.BlockSpec((None, ts, d), lambda b, i: (b, i, 0)))
        out_shape.append(jax.ShapeDtypeStruct((bsz, s, d), BF16))
    if moe:
        in_specs.append(_const_spec((None, 2 * N_EXPERTS, d), lambda b, i: (l // 2, 0, 0)))
        args.append(rw_t)
        out_specs += [pl.BlockSpec((2, ts), lambda b, i: (0, b * ns + i)),
                      pl.BlockSpec((2, ts), lambda b, i: (0, b * ns + i))]
        out_shape += [jax.ShapeDtypeStruct((2, t), jnp.int32), jax.ShapeDtypeStruct((2, t), F32)]
    scratch = [
        pltpu.VMEM((ts, d), BF16),
        pltpu.VMEM((d // LANES, ts, LANES), F32),
        pltpu.VMEM((ts, d), BF16),
        pltpu.VMEM((ts, d), BF16),
        pltpu.VMEM((ts, d), F32),
        pltpu.VMEM((ts, d), F32),
        pltpu.VMEM((ts, d), F32),
        pltpu.VMEM((ts, d), F32),
        pltpu.VMEM((d // LANES, ts, LANES), F32),
        pltpu.VMEM((ts, d), BF16),
        pltpu.VMEM((ts, d), BF16),
        pltpu.VMEM((ts + HALO * SUBLANES, d), F32),
        pltpu.VMEM((HALO * SUBLANES, d), F32),
        pltpu.VMEM((ts, d), BF16),
        pltpu.VMEM((ts, d), BF16),
        pltpu.VMEM((ts, d), BF16),
        pltpu.VMEM((ts, d), BF16),
        pltpu.VMEM((GM_GROUPS, CHUNK, CHUNK), BF16),
    ]
    if moe:
        scratch.append(pltpu.VMEM((ts, d), F32))
    return pl.pallas_call(
        functools.partial(_mixer_kernel, moe, ts, d),
        grid=(bsz, ns),
        in_specs=in_specs,
        out_specs=out_specs,
        out_shape=out_shape,
        scratch_shapes=scratch,
        compiler_params=pltpu.CompilerParams(
            dimension_semantics=("arbitrary", "arbitrary"), vmem_limit_bytes=VMEM_LIMIT_BYTES),
        name="mixer_moe" if moe else "mixer_dense",
    )(*args)


def _swiglu_tile(x_ref, w1_ref, w3_ref, w2_ref, g_buf, fc):
    f = w1_ref.shape[-1]
    x = x_ref[...]
    for c0 in range(0, f, fc):
        cols = slice(c0, c0 + fc)
        h1 = jnp.dot(x, w1_ref[:, cols], preferred_element_type=F32)
        h3 = jnp.dot(x, w3_ref[:, cols], preferred_element_type=F32)
        g_buf[:, cols] = (jax.nn.silu(h1) * h3).astype(BF16)
    return jnp.dot(g_buf[...], w2_ref[...], preferred_element_type=F32)


def _ffn_dense_kernel(fc, x_ref, h_ref, mod_ref, w1_ref, w3_ref, w2_ref, o_ref, g_buf):
    g2 = mod_ref[5:6, :]
    o_ref[...] = x_ref[...] + g2 * _swiglu_tile(h_ref, w1_ref, w3_ref, w2_ref, g_buf, fc)


def _ffn_dense(l, x, h2, mod, w1, w3, w2):
    t, d = x.shape
    f = w1.shape[-1]
    s = t // mod.shape[1]
    tm = min(FFN_ROWS, s)
    j = l // 2
    return pl.pallas_call(
        functools.partial(_ffn_dense_kernel, FFN_COLS_DENSE),
        grid=(t // tm,),
        in_specs=[
            pl.BlockSpec((tm, d), lambda i: (i, 0)),
            pl.BlockSpec((tm, d), lambda i: (i, 0)),
            pl.BlockSpec((None, None, 6, d), lambda i: (l, (i * tm) // s, 0, 0)),
            _const_spec((None, d, f), lambda i: (j, 0, 0)),
            _const_spec((None, d, f), lambda i: (j, 0, 0)),
            _const_spec((None, f, d), lambda i: (j, 0, 0)),
        ],
        out_specs=pl.BlockSpec((tm, d), lambda i: (i, 0)),
        out_shape=jax.ShapeDtypeStruct((t, d), F32),
        scratch_shapes=[pltpu.VMEM((tm, f), BF16)],
        compiler_params=pltpu.CompilerParams(
            dimension_semantics=("arbitrary",), vmem_limit_bytes=VMEM_LIMIT_BYTES),
        name="ffn_dense",
    )(x, h2, mod, w1, w3, w2)


def _token_copy(src, src_row, dst, dst_row, sem):
    return pltpu.make_async_copy(src.at[pl.ds(pl.multiple_of(src_row, SUBLANES), SUBLANES), :],
                                 dst.at[pl.ds(pl.multiple_of(dst_row, SUBLANES), SUBLANES), :], sem)


def _ffn_moe_kernel(tm, fc, nt, te_ref, nu_ref, idx_hbm, h_hbm, w1_ref, w3_ref, w2_ref, y_hbm,
                    idx_smem, xbuf, xbf, g_buf, ybuf, idx_sem, in_sem, out_sem):
    i = pl.program_id(0)
    n_used = nu_ref[0]
    slot = lax.rem(i, 2)
    d = xbf.shape[1]

    def idx_base(tile):
        return pl.multiple_of(lax.rem(tile, IDX_SLOTS) * (2 * tm), 2 * tm)

    def idx_copy(tile, ring_tile=None):
        ring_tile = tile if ring_tile is None else ring_tile
        return pltpu.make_async_copy(idx_hbm.at[tile], idx_smem.at[pl.ds(idx_base(ring_tile), 2 * tm)],
                                     idx_sem.at[lax.rem(ring_tile, IDX_SLOTS)])

    def gather_copy(tile, sl, r):
        return _token_copy(h_hbm, idx_smem[idx_base(tile) + r], xbuf.at[sl], r * SUBLANES, in_sem.at[sl])

    def scatter_copy(tile, sl, r):
        return _token_copy(ybuf.at[sl], r * SUBLANES, y_hbm, idx_smem[idx_base(tile) + tm + r],
                           out_sem.at[sl])

    def wait_gathers(sl):
        pltpu.make_async_copy(h_hbm.at[pl.ds(0, tm * SUBLANES), :], xbuf.at[sl], in_sem.at[sl]).wait()

    def wait_scatters(sl):
        pltpu.make_async_copy(ybuf.at[sl], y_hbm.at[pl.ds(0, tm * SUBLANES), :], out_sem.at[sl]).wait()

    @pl.when(i == 0)
    def _():
        idx_copy(0).start()
        idx_copy(0).wait()
        def body(r, carry):
            gather_copy(0, 0, r).start()
            return carry
        lax.fori_loop(0, tm, body, 0, unroll=DMA_UNROLL)
        idx_copy(1).start()
        idx_copy(nt - 1, IDX_SLOTS - 1).start()
        idx_copy(nt - 1, IDX_SLOTS - 1).wait()
        ybuf[1] = jnp.zeros(ybuf.shape[1:], F32)

    @pl.when(i < n_used)
    def _():
        wait_gathers(slot)
        @pl.when(i + 2 <= n_used)
        def _():
            idx_copy(i + 2).start()
        idx_copy(i + 1).wait()
        for s8 in range(d // LANES):
            xbf[:, s8 * LANES:(s8 + 1) * LANES] = (
                xbuf[slot, pl.ds(s8, tm, stride=SUBLANES), :].astype(BF16))
        for r in range(tm):
            gather_copy(i + 1, 1 - slot, r).start()
            scatter_copy(i + IDX_SLOTS - 1, 1 - slot, r).start()
        y = _swiglu_tile(xbf, w1_ref, w3_ref, w2_ref, g_buf, fc)
        for s8 in range(d // LANES):
            ybuf[slot, pl.ds(s8, tm, stride=SUBLANES), :] = y[:, s8 * LANES:(s8 + 1) * LANES]
        wait_scatters(1 - slot)
        @pl.when(i == n_used - 1)
        def _():
            wait_gathers(1 - slot)
            def body(r, carry):
                scatter_copy(i, slot, r).start()
                return carry
            lax.fori_loop(0, tm, body, 0, unroll=DMA_UNROLL)
            wait_scatters(slot)


def _ffn_moe(jl, h2t, idx, tile_expert, n_used, w1, w3, w2, tm, nt, n_out_rows):
    d, f = w1.shape[-2:]
    grid_spec = pltpu.PrefetchScalarGridSpec(
        num_scalar_prefetch=2,
        grid=(nt,),
        in_specs=[
            pl.BlockSpec(memory_space=pl.ANY),
            pl.BlockSpec(memory_space=pl.ANY),
            _const_spec((None, None, d, f), lambda i, te, nu: (jl, te[i], 0, 0)),
            _const_spec((None, None, d, f), lambda i, te, nu: (jl, te[i], 0, 0)),
            _const_spec((None, None, f, d), lambda i, te, nu: (jl, te[i], 0, 0)),
        ],
        out_specs=pl.BlockSpec(memory_space=pl.ANY),
        scratch_shapes=[
            pltpu.SMEM((IDX_SLOTS * 2 * tm,), jnp.int32),
            pltpu.VMEM((2, tm * SUBLANES, LANES), F32),
            pltpu.VMEM((tm, d), BF16),
            pltpu.VMEM((tm, f), BF16),
            pltpu.VMEM((2, tm * SUBLANES, LANES), F32),
            pltpu.SemaphoreType.DMA((IDX_SLOTS,)), pltpu.SemaphoreType.DMA((2,)),
            pltpu.SemaphoreType.DMA((2,)),
        ],
    )
    return pl.pallas_call(
        functools.partial(_ffn_moe_kernel, tm, FFN_COLS_MOE, nt),
        grid_spec=grid_spec,
        out_shape=jax.ShapeDtypeStruct((n_out_rows, LANES), F32),
        compiler_params=pltpu.CompilerParams(
            dimension_semantics=("arbitrary",), vmem_limit_bytes=VMEM_LIMIT_BYTES,
            disable_bounds_checks=True, has_side_effects=True),
        name="ffn_moe",
    )(tile_expert, n_used, idx, h2t, w1, w3, w2)


def _combine_kernel(tc, d, final, x_ref, y0_ref, y1_ref, tw_ref, mod_ref, fg_ref, o_ref):
    g2 = mod_ref[5:6, :]
    tw = tw_ref[...]
    parts = []
    for s8 in range(d // LANES):
        y0 = y0_ref[pl.ds(s8, tc, stride=SUBLANES), :]
        y1 = y1_ref[pl.ds(s8, tc, stride=SUBLANES), :]
        parts.append(tw[:, 0:1] * y0 + tw[:, 1:2] * y1)
    xn = x_ref[...] + g2 * jnp.concatenate(parts, axis=1)
    if final:
        xn = _rms(xn) * fg_ref[...]
    o_ref[...] = xn


def _combine(l, final, x, y2, tw_t, mod, final_g):
    t, d = x.shape
    s = t // mod.shape[1]
    tc = min(COMBINE_ROWS, s)
    nt = t // tc
    return pl.pallas_call(
        functools.partial(_combine_kernel, tc, d, final),
        grid=(nt,),
        in_specs=[
            pl.BlockSpec((tc, d), lambda i: (i, 0)),
            pl.BlockSpec((tc * SUBLANES, LANES), lambda i: (i, 0)),
            pl.BlockSpec((tc * SUBLANES, LANES), lambda i: (nt + i, 0)),
            pl.BlockSpec((tc, 2), lambda i: (i, 0)),
            pl.BlockSpec((None, None, 6, d), lambda i: (l, (i * tc) // s, 0, 0)),
            pl.BlockSpec((1, d), lambda i: (0, 0)),
        ],
        out_specs=pl.BlockSpec((tc, d), lambda i: (i, 0)),
        out_shape=jax.ShapeDtypeStruct((t, d), F32),
        compiler_params=pltpu.CompilerParams(
            dimension_semantics=("arbitrary",), vmem_limit_bytes=VMEM_LIMIT_BYTES),
        name="moe_combine",
    )(x, y2, y2, tw_t, mod, final_g.reshape(1, d))


def _dispatch_plan(ti, tm, nt):
    t = ti.shape[1]
    e_flat = ti.reshape(-1)
    experts = jnp.arange(N_EXPERTS, dtype=jnp.int32)
    counts = jnp.sum((e_flat[:, None] == experts[None, :]).astype(jnp.int32), axis=0)
    order = jnp.argsort(e_flat, stable=True).astype(jnp.int32)
    group_start = jnp.cumsum(counts) - counts
    tiles_e = (counts + tm - 1) // tm
    tile_end = jnp.cumsum(tiles_e)
    tile_start = tile_end - tiles_e
    n_used = tile_end[-1]
    tile_id = jnp.minimum(jnp.arange(nt, dtype=jnp.int32), n_used - 1)
    tile_expert = jnp.minimum(
        jnp.searchsorted(tile_end, tile_id, side="right").astype(jnp.int32), N_EXPERTS - 1)
    first_row = (jnp.arange(nt, dtype=jnp.int32) - tile_start[tile_expert]) * tm
    n_valid = jnp.clip(counts[tile_expert] - first_row, 0, tm)
    n_valid = jnp.where(jnp.arange(nt) < n_used, n_valid, 0).astype(jnp.int32)
    p = first_row[:, None] + jnp.arange(tm, dtype=jnp.int32)[None, :]
    valid = jnp.arange(tm, dtype=jnp.int32)[None, :] < n_valid[:, None]
    a = order[jnp.clip(group_start[tile_expert][:, None] + p, 0, 2 * t - 1)]
    src_row = jnp.where(valid, a % t, 0) * SUBLANES
    dump = 2 * t + jnp.arange(tm, dtype=jnp.int32)[None, :]
    dst_row = jnp.where(valid, a, dump) * SUBLANES
    idx = jnp.concatenate([src_row, dst_row], axis=1).astype(jnp.int32)
    return idx, tile_expert, n_used.reshape(1).astype(jnp.int32)


def kernel(x, c, ada_w, ada_b, w_in, gm_ln_g, gm_ln_b, w_sp, b_sp, w_gm_out, conv_w, conv_b,
           cv_ln_g, cv_ln_b, w_cv_out, w_o, ffn_w1, ffn_w3, ffn_w2, router_w, exp_w1, exp_w3,
           exp_w2, final_g):
    bsz, s, d = x.shape
    depth = ada_w.shape[0]
    t = bsz * s
    assert s % CHUNK == 0 and d % GM_GROUPS == 0 and depth % 2 == 0

    mod = _ada_modulation(c, ada_w, ada_b).reshape(depth, bsz, 6, d)

    win = w_in.astype(BF16)
    wgm, wcv, wo = w_gm_out.astype(BF16), w_cv_out.astype(BF16), w_o.astype(BF16)
    bsp_full = jnp.repeat(jnp.swapaxes(b_sp, 1, 2), d // GM_GROUPS, axis=2)
    zeros = jnp.zeros_like(conv_b)
    vecs = jnp.stack([gm_ln_g, gm_ln_b, cv_ln_g, cv_ln_b, conv_b, zeros, zeros, zeros], axis=1)
    cw = jnp.pad(conv_w, ((0, 0), (0, HALO - CONV_K), (0, 0)))
    cw = jnp.repeat(cw, SUBLANES, axis=1)
    rw_t = jnp.swapaxes(router_w, 1, 2)
    rw_hi = rw_t.astype(BF16)
    rw_t = jnp.concatenate([rw_hi, (rw_t - rw_hi.astype(F32)).astype(BF16)], axis=1)
    fw1, fw3, fw2 = ffn_w1.astype(BF16), ffn_w3.astype(BF16), ffn_w2.astype(BF16)
    ew1, ew3, ew2 = exp_w1.astype(BF16), exp_w3.astype(BF16), exp_w2.astype(BF16)

    tm = min(FFN_ROWS, t)
    nt = (2 * t) // tm + N_EXPERTS

    for l in range(depth):
        moe = l % 2 == 1
        outs = _mixer(l, moe, x, mod, win, w_sp, bsp_full, vecs, cw, wgm, wcv, wo, rw_t)
        if not moe:
            xn, h2 = outs
            x2 = _ffn_dense(l, xn.reshape(t, d), h2.reshape(t, d), mod, fw1, fw3, fw2)
        else:
            xn, h2t, ti, tw = outs
            idx, tile_expert, n_used = _dispatch_plan(ti, tm, nt)
            y2 = _ffn_moe(l // 2, h2t, idx, tile_expert, n_used, ew1, ew3, ew2, tm, nt,
                          (2 * t + tm) * SUBLANES)
            x2 = _combine(l, l == depth - 1, xn.reshape(t, d), y2, tw.T, mod, final_g)
        x = x2.reshape(bsz, s, d)
    return x
```
